```python
import jax
import jax.numpy as jnp
from jax import lax
import numpy as np

D_MODEL = 1024
BATCH = 8
SEQ = 4096
DEPTH = 4

GRID_W = 64
HEAD_DIM = 64
LRU_BLOCKS = 6
LRU_BLOCK_DIM = 64
LRU_WIDTH = LRU_BLOCKS * LRU_BLOCK_DIM
RET_HEADS = 6
RET_WIDTH = RET_HEADS * HEAD_DIM
NA_HEADS = 4
NA_WIDTH = NA_HEADS * HEAD_DIM
MIX_WIDTH = LRU_WIDTH + RET_WIDTH + NA_WIDTH
IN_WIDTH = 2 * LRU_WIDTH + 4 * RET_WIDTH + 3 * NA_WIDTH
CONV_WIDTH = 4
LRU_C = 8.0
LRU_A_MIN = 0.9
LRU_A_MAX = 0.999
RET_CHUNK = 128
ROPE_BASE = 10000.0
GN_EPS = 1e-6
NA_MAX_KH = 8
NA_KW = 16
NA_QB = 16
NA_KS = NA_QB + NA_KW
D_FF = -(-8 * D_MODEL // (3 * 256)) * 256
DEEPNORM_ALPHA = (2 * DEPTH) ** 0.25
DEEPNORM_BETA = (8 * DEPTH) ** -0.25
LN_EPS = 1e-5

kernel_name = "hybrid_lru_retention_natten_deepnorm_encoder"


def _layer_norm(x, g, b):
    xf = x.astype(jnp.float32)
    mu = jnp.mean(xf, axis=-1, keepdims=True)
    xc = xf - mu
    var = jnp.mean(xc * xc, axis=-1, keepdims=True)
    y = xc * lax.rsqrt(var + LN_EPS) * g.astype(jnp.float32) + b.astype(jnp.float32)
    return y.astype(x.dtype)


def _linear_combine(c1, c2):
    a1, b1 = c1
    a2, b2 = c2
    return a1 * a2, a2 * b1 + b2


def _rg_lru_group(xb, gate, conv_w, conv_b, w_a, b_a, w_x, b_x, lam):
    bsz, seq, _ = xb.shape
    left = CONV_WIDTH // 2
    xp = jnp.pad(xb.astype(jnp.float32), ((0, 0), (left, CONV_WIDTH - 1 - left), (0, 0)))
    cw = conv_w.astype(jnp.float32)
    xc = conv_b.astype(jnp.float32)
    for j in range(CONV_WIDTH):
        xc = xc + xp[:, j:j + seq] * cw[j]
    xg = xc.reshape(bsz, seq, LRU_BLOCKS, LRU_BLOCK_DIM)
    h_sum = jnp.zeros_like(xc)
    for d, rev in enumerate((False, True)):
        r = jax.nn.sigmoid(jnp.einsum('bsgi,gij->bsgj', xg, w_a[d].astype(jnp.float32)).reshape(bsz, seq, LRU_WIDTH) + b_a[d].astype(jnp.float32))
        i = jax.nn.sigmoid(jnp.einsum('bsgi,gij->bsgj', xg, w_x[d].astype(jnp.float32)).reshape(bsz, seq, LRU_WIDTH) + b_x[d].astype(jnp.float32))
        log_a = -LRU_C * r * jax.nn.softplus(-lam[d].astype(jnp.float32))
        a = jnp.exp(log_a)
        u = jnp.sqrt(-jnp.expm1(2.0 * log_a)) * (i * xc)
        _, h = lax.associative_scan(_linear_combine, (a, u), axis=1, reverse=rev)
        h_sum = h_sum + h
    return (h_sum * jax.nn.gelu(gate.astype(jnp.float32))).astype(xb.dtype)


def _rope(t, pos):
    half = HEAD_DIM // 2
    inv_freq = ROPE_BASE ** (-jnp.arange(half, dtype=jnp.float32) / half)
    ang = pos[:, None] * inv_freq[None, :]
    cos = jnp.cos(ang)[None, :, None, :]
    sin = jnp.sin(ang)[None, :, None, :]
    t1, t2 = t[..., :half], t[..., half:]
    return jnp.concatenate([t1 * cos - t2 * sin, t1 * sin + t2 * cos], axis=-1)


def _retention_group(q, k, v, g, gn_w):
    bsz, seq, _ = q.shape
    C = RET_CHUNK
    nc = seq // C
    pos = jnp.arange(seq, dtype=jnp.float32)

    def heads(t):
        return t.astype(jnp.float32).reshape(bsz, seq, RET_HEADS, HEAD_DIM)

    def chunks(t):
        return t.transpose(0, 2, 1, 3).reshape(bsz, RET_HEADS, nc, C, HEAD_DIM)

    qc = chunks(_rope(heads(q), pos))
    kc = chunks(_rope(heads(k), pos) * HEAD_DIM ** -0.5)
    vc = chunks(heads(v))
    log_g = jnp.log1p(-jnp.exp2(-5.0 - jnp.arange(RET_HEADS, dtype=jnp.float32)))
    lg = log_g[:, None]
    idx = jnp.arange(C, dtype=jnp.float32)

    def bc(w):
        return w[:, None, :, None]

    decay = jnp.exp(jnp.abs(idx[:, None] - idx[None, :]) * log_g[:, None, None])
    s = jnp.einsum('bhnid,bhnjd->bhnij', qc, kc) * decay[:, None]
    o = jnp.einsum('bhnij,bhnjd->bhnid', s, vc)
    kv_f = jnp.einsum('bhnjd,bhnje->nbhde', kc * bc(jnp.exp((C - 1 - idx) * lg)), vc)
    kv_b = jnp.einsum('bhnjd,bhnje->nbhde', kc * bc(jnp.exp(idx * lg)), vc)
    g_chunk = jnp.exp(C * log_g)[:, None, None]

    def step(state, kv):
        return g_chunk * state + kv, state

    init = jnp.zeros((bsz, RET_HEADS, HEAD_DIM, HEAD_DIM), jnp.float32)
    _, st_f = lax.scan(step, init, kv_f)
    _, st_b = lax.scan(step, init, kv_b, reverse=True)
    o = (o
         + jnp.einsum('bhnid,nbhde->bhnie', qc * bc(jnp.exp((idx + 1.0) * lg)), st_f)
         + jnp.einsum('bhnid,nbhde->bhnie', qc * bc(jnp.exp((C - idx) * lg)), st_b))
    mu = jnp.mean(o, axis=-1, keepdims=True)
    oc = o - mu
    var = jnp.mean(oc * oc, axis=-1, keepdims=True)
    o = oc * lax.rsqrt(var + GN_EPS)
    o = o.reshape(bsz, RET_HEADS, seq, HEAD_DIM).transpose(0, 2, 1, 3).reshape(bsz, seq, RET_WIDTH)
    o = o * gn_w.astype(jnp.float32)
    return (jax.nn.silu(g.astype(jnp.float32)) * o).astype(q.dtype)


def _neighbourhood_attention_group(q, k, v, rpb):
    bsz, seq, _ = q.shape
    rows_n = seq // GRID_W
    kh = min(NA_MAX_KH, rows_n)
    nb = GRID_W // NA_QB
    rows = np.arange(rows_n)
    rstart = np.clip(rows - kh // 2, 0, rows_n - kh)
    row_idx = rstart[:, None] + np.arange(kh)
    dr = row_idx - rows[:, None]
    c0 = np.arange(nb) * NA_QB
    sstart = np.clip(c0 - NA_KW // 2, 0, GRID_W - NA_KS)
    col_idx = sstart[:, None] + np.arange(NA_KS)
    qcol = c0[:, None] + np.arange(NA_QB)
    cstart = np.clip(qcol - NA_KW // 2, 0, GRID_W - NA_KW)
    dc = col_idx[:, None, :] - qcol[:, :, None]
    valid = (col_idx[:, None, :] >= cstart[:, :, None]) & (col_idx[:, None, :] < cstart[:, :, None] + NA_KW)
    ri = (dr + NA_MAX_KH - 1)[:, None, None, :, None]
    ci = (np.clip(dc, 1 - NA_KW, NA_KW - 1) + NA_KW - 1)[None, :, :, None, :]
    bias = rpb.astype(jnp.float32)[:, ri, ci]
    bias = jnp.where(valid[None, None, :, :, None, :], bias, -jnp.inf)
    bias = bias.reshape(NA_HEADS, rows_n, nb, NA_QB, kh * NA_KS)

    def heads(t):
        return t.reshape(bsz, seq, NA_HEADS, HEAD_DIM).transpose(0, 2, 1, 3)

    qg = heads(q).reshape(bsz, NA_HEADS, rows_n, nb, NA_QB, HEAD_DIM)
    kgrid = heads(k).reshape(bsz, NA_HEADS, rows_n, GRID_W, HEAD_DIM)
    vgrid = heads(v).reshape(bsz, NA_HEADS, rows_n, GRID_W, HEAD_DIM)
    gr = row_idx[:, None, :, None]
    gc = col_idx[None, :, None, :]
    kb = kgrid[:, :, gr, gc].reshape(bsz, NA_HEADS, rows_n, nb, kh * NA_KS, HEAD_DIM)
    vb = vgrid[:, :, gr, gc].reshape(bsz, NA_HEADS, rows_n, nb, kh * NA_KS, HEAD_DIM)
    s = jnp.einsum('bhrnqd,bhrnkd->bhrnqk', qg, kb).astype(jnp.float32) * HEAD_DIM ** -0.5 + bias[None]
    p = jax.nn.softmax(s, axis=-1).astype(v.dtype)
    o = jnp.einsum('bhrnqk,bhrnkd->bhrnqd', p, vb)
    return o.reshape(bsz, NA_HEADS, seq, HEAD_DIM).transpose(0, 2, 1, 3).reshape(bsz, seq, NA_WIDTH)


def setup_inputs(seed: int = 0) -> dict:
    key = jax.random.key(seed)
    ks = jax.random.split(key, 20)

    def nrm(k, shape, scale):
        return jax.random.normal(k, shape, jnp.float32) * scale

    x = nrm(ks[0], (BATCH, SEQ, D_MODEL), 1.0)
    w_in = nrm(ks[1], (DEPTH, D_MODEL, IN_WIDTH), D_MODEL ** -0.5)
    conv_w = nrm(ks[2], (DEPTH, CONV_WIDTH, LRU_WIDTH), CONV_WIDTH ** -0.5)
    conv_b = nrm(ks[3], (DEPTH, LRU_WIDTH), 0.01)
    lru_w_a = nrm(ks[4], (DEPTH, 2, LRU_BLOCKS, LRU_BLOCK_DIM, LRU_BLOCK_DIM), LRU_BLOCK_DIM ** -0.5)
    lru_b_a = nrm(ks[5], (DEPTH, 2, LRU_WIDTH), 0.01)
    lru_w_x = nrm(ks[6], (DEPTH, 2, LRU_BLOCKS, LRU_BLOCK_DIM, LRU_BLOCK_DIM), LRU_BLOCK_DIM ** -0.5)
    lru_b_x = nrm(ks[7], (DEPTH, 2, LRU_WIDTH), 0.01)
    a_pow = jax.random.uniform(ks[8], (DEPTH, 2, LRU_WIDTH), jnp.float32, minval=LRU_A_MIN, maxval=LRU_A_MAX)
    a0 = a_pow ** (1.0 / LRU_C)
    lru_lam = jnp.log(a0) - jnp.log1p(-a0)
    ret_gn_w = 1.0 + nrm(ks[9], (DEPTH, RET_WIDTH), 0.02)
    na_rpb = nrm(ks[10], (DEPTH, NA_HEADS, 2 * NA_MAX_KH - 1, 2 * NA_KW - 1), 0.02)
    w_out = nrm(ks[11], (DEPTH, MIX_WIDTH, D_MODEL), MIX_WIDTH ** -0.5 * DEEPNORM_BETA)
    ln1_g = 1.0 + nrm(ks[12], (DEPTH, D_MODEL), 0.02)
    ln1_b = nrm(ks[13], (DEPTH, D_MODEL), 0.01)
    w_gate = nrm(ks[14], (DEPTH, D_MODEL, D_FF), D_MODEL ** -0.5)
    w_up = nrm(ks[15], (DEPTH, D_MODEL, D_FF), D_MODEL ** -0.5)
    w_down = nrm(ks[16], (DEPTH, D_FF, D_MODEL), D_FF ** -0.5 * DEEPNORM_BETA)
    ln2_g = 1.0 + nrm(ks[17], (DEPTH, D_MODEL), 0.02)
    ln2_b = nrm(ks[18], (DEPTH, D_MODEL), 0.01)
    return {'x': x, 'w_in': w_in, 'conv_w': conv_w, 'conv_b': conv_b,
            'lru_w_a': lru_w_a, 'lru_b_a': lru_b_a, 'lru_w_x': lru_w_x, 'lru_b_x': lru_b_x,
            'lru_lam': lru_lam, 'ret_gn_w': ret_gn_w, 'na_rpb': na_rpb, 'w_out': w_out,
            'ln1_g': ln1_g, 'ln1_b': ln1_b, 'w_gate': w_gate, 'w_up': w_up, 'w_down': w_down,
            'ln2_g': ln2_g, 'ln2_b': ln2_b}


def reference(x, w_in, conv_w, conv_b, lru_w_a, lru_b_a, lru_w_x, lru_b_x, lru_lam,
              ret_gn_w, na_rpb, w_out, ln1_g, ln1_b, w_gate, w_up, w_down, ln2_g, ln2_b):
    sizes = [LRU_WIDTH, LRU_WIDTH, RET_WIDTH, RET_WIDTH, RET_WIDTH, RET_WIDTH, NA_WIDTH, NA_WIDTH, NA_WIDTH]
    offsets = [int(o) for o in np.cumsum(sizes)[:-1]]
    for l in range(DEPTH):
        proj = x @ w_in[l]
        lru_x, lru_g, rq, rk, rv, rg, nq, nk, nv = jnp.split(proj, offsets, axis=-1)
        y_lru = _rg_lru_group(lru_x, lru_g, conv_w[l], conv_b[l], lru_w_a[l], lru_b_a[l],
                              lru_w_x[l], lru_b_x[l], lru_lam[l]).astype(proj.dtype)
        y_ret = _retention_group(rq, rk, rv, rg, ret_gn_w[l]).astype(proj.dtype)
        y_na = _neighbourhood_attention_group(nq, nk, nv, na_rpb[l]).astype(proj.dtype)
        mix = jnp.concatenate([y_lru, y_ret, y_na], axis=-1) @ w_out[l]
        x = _layer_norm(DEEPNORM_ALPHA * x + mix, ln1_g[l], ln1_b[l])
        hid = jax.nn.silu(x @ w_gate[l]) * (x @ w_up[l])
        x = _layer_norm(DEEPNORM_ALPHA * x + hid @ w_down[l], ln2_g[l], ln2_b[l])
    return x
```

```python
import functools
import math

import numpy as np
import jax
import jax.numpy as jnp
from jax import lax
from jax.experimental import pallas as pl
from jax.experimental.pallas import tpu as pltpu

F32 = jnp.float32
BF16 = jnp.bfloat16

LANES = 128
SUBLANES = 8
VMEM_LIMIT_BYTES = 56 * 1024 * 1024

GRID_W = 64
HEAD_DIM = 64
LRU_WIDTH = 384
RET_HEADS = 6
RET_WIDTH = 384
NA_HEADS = 4
NA_WIDTH = 256
CONV_WIDTH = 4
LRU_C = 8.0
ROPE_BASE = 10000.0
GN_EPS = 1e-6
NA_MAX_KH = 8
NA_KW = 16
LN_EPS = 1e-5
CHUNK = 128
HALO = SUBLANES

LRU_GROUPS = LRU_WIDTH // LANES
RET_GROUPS = RET_WIDTH // LANES


def _dot(a, b):
    return jnp.dot(a, b, preferred_element_type=F32)


def _dot_nt(a, b):
    return lax.dot_general(a, b, (((1,), (1,)), ((), ())), preferred_element_type=F32)


def _dot_tn(a, b):
    return lax.dot_general(a, b, (((0,), (0,)), ((), ())), preferred_element_type=F32)


def _params(*sem):
    return pltpu.CompilerParams(dimension_semantics=sem, vmem_limit_bytes=VMEM_LIMIT_BYTES)


def _sigmoid(z):
    return 1.0 / (1.0 + jnp.exp(-z))


def _in_proj_kernel(x_ref, w_ref, lru_ref, ret_ref, na_ref):
    xb = x_ref[...].astype(BF16)
    n_lru = lru_ref.shape[1]
    n_ret = ret_ref.shape[1]
    lru_ref[...] = _dot(xb, w_ref[:, :n_lru])
    ret_ref[...] = _dot(xb, w_ref[:, n_lru:n_lru + n_ret])
    na_ref[...] = _dot(xb, w_ref[:, n_lru + n_ret:]).astype(na_ref.dtype)


def _in_proj(x2, w_bf, tm):
    n, d = x2.shape
    n_lru, n_ret, n_na = 2 * LRU_WIDTH, 4 * RET_WIDTH, 3 * NA_WIDTH
    return pl.pallas_call(
        _in_proj_kernel,
        grid=(n // tm,),
        in_specs=[pl.BlockSpec((tm, d), lambda i: (i, 0)),
                  pl.BlockSpec(w_bf.shape, lambda i: (0, 0))],
        out_specs=[pl.BlockSpec((tm, n_lru), lambda i: (i, 0)),
                   pl.BlockSpec((tm, n_ret), lambda i: (i, 0)),
                   pl.BlockSpec((tm, n_na), lambda i: (i, 0))],
        out_shape=[jax.ShapeDtypeStruct((n, n_lru), F32),
                   jax.ShapeDtypeStruct((n, n_ret), F32),
                   jax.ShapeDtypeStruct((n, n_na), BF16)],
        compiler_params=_params("parallel"),
        name="in_proj",
    )(x2, w_bf)


LRU_ROW_TILE = 256


def _lru_gates(xc, wg_ref, ba_ref, bx_ref, nsp_ref, g):
    z = _dot(xc.astype(BF16), wg_ref[g])
    r = _sigmoid(z[:, :LANES] + ba_ref[:, g * LANES:(g + 1) * LANES])
    i = _sigmoid(z[:, LANES:] + bx_ref[:, g * LANES:(g + 1) * LANES])
    log_a = r * nsp_ref[:, g * LANES:(g + 1) * LANES]
    a = jnp.exp(log_a)
    u = jnp.sqrt(1.0 - a * a) * (i * xc)
    return a, u


def _lru_fwd_kernel(x_ref, prev_ref, next_ref, cw_ref, cb_ref, wg_ref, ba_ref, bx_ref, nsp_ref,
                    xc_ref, hf_ref, xs_scr, a_scr, h_scr):
    i = pl.program_id(0)
    nblk = pl.num_programs(0)
    nb, tc, _ = x_ref.shape
    rows = tc * nb
    halo_rows = HALO * nb

    @pl.when(i == 0)
    def _():
        h_scr[...] = jnp.zeros_like(h_scr)

    has_prev = (i > 0).astype(F32)
    has_next = (i < nblk - 1).astype(F32)
    for g in range(LRU_GROUPS):
        lanes = slice(g * LANES, (g + 1) * LANES)
        for b in range(nb):
            xs_scr[g, pl.ds(b, HALO, stride=nb), :] = prev_ref[b, :, lanes] * has_prev
            xs_scr[g, pl.ds(halo_rows + b, tc, stride=nb), :] = x_ref[b, :, lanes]
            xs_scr[g, pl.ds(halo_rows + rows + b, HALO, stride=nb), :] = next_ref[b, :, lanes] * has_next

    left = CONV_WIDTH // 2
    for g in range(LRU_GROUPS):
        lanes = slice(g * LANES, (g + 1) * LANES)

        def tile(k, carry, g=g, lanes=lanes):
            r0 = pl.multiple_of(k * LRU_ROW_TILE, LRU_ROW_TILE)
            xc = jnp.broadcast_to(cb_ref[:, lanes], (LRU_ROW_TILE, LANES))
            for j in range(CONV_WIDTH):
                off = halo_rows + (j - left) * nb
                xc = xc + xs_scr[g, pl.ds(r0 + off, LRU_ROW_TILE), :] * cw_ref[j:j + 1, lanes]
            a, u = _lru_gates(xc, wg_ref, ba_ref, bx_ref, nsp_ref, g)
            xc_ref[pl.ds(r0, LRU_ROW_TILE), lanes] = xc
            a_scr[pl.ds(r0, LRU_ROW_TILE), lanes] = a
            hf_ref[pl.ds(r0, LRU_ROW_TILE), lanes] = u
            return carry

        lax.fori_loop(0, rows // LRU_ROW_TILE, tile, 0)

    def step(t, h):
        r0 = pl.multiple_of(t * nb, nb)
        h = a_scr[pl.ds(r0, nb), :] * h + hf_ref[pl.ds(r0, nb), :]
        hf_ref[pl.ds(r0, nb), :] = h
        return h

    h_scr[...] = lax.fori_loop(0, tc, step, h_scr[...], unroll=8)


def _lru_bwd_kernel(xc_ref, hf_ref, gate_ref, wg_ref, ba_ref, bx_ref, nsp_ref,
                    y_ref, a_scr, hb_scr, h_scr):
    i = pl.program_id(0)
    nb, tc, _ = gate_ref.shape
    rows = tc * nb

    @pl.when(i == 0)
    def _():
        h_scr[...] = jnp.zeros_like(h_scr)

    for g in range(LRU_GROUPS):
        lanes = slice(g * LANES, (g + 1) * LANES)

        def tile(k, carry, g=g, lanes=lanes):
            r0 = pl.multiple_of(k * LRU_ROW_TILE, LRU_ROW_TILE)
            xc = xc_ref[pl.ds(r0, LRU_ROW_TILE), lanes]
            a, u = _lru_gates(xc, wg_ref, ba_ref, bx_ref, nsp_ref, g)
            a_scr[pl.ds(r0, LRU_ROW_TILE), lanes] = a
            hb_scr[g, pl.ds(r0, LRU_ROW_TILE), :] = u
            return carry

        lax.fori_loop(0, rows // LRU_ROW_TILE, tile, 0)

    def step(s, h):
        r0 = pl.multiple_of((tc - 1 - s) * nb, nb)
        new = []
        for g in range(LRU_GROUPS):
            lanes = slice(g * LANES, (g + 1) * LANES)
            hg = a_scr[pl.ds(r0, nb), lanes] * h[g] + hb_scr[g, pl.ds(r0, nb), :]
            hb_scr[g, pl.ds(r0, nb), :] = hg + hf_ref[pl.ds(r0, nb), lanes]
            new.append(hg)
        return tuple(new)

    h0 = tuple(h_scr[g] for g in range(LRU_GROUPS))
    hl = lax.fori_loop(0, tc, step, h0, unroll=8)
    for g in range(LRU_GROUPS):
        h_scr[g] = hl[g]

    for g in range(LRU_GROUPS):
        lanes = slice(g * LANES, (g + 1) * LANES)
        for b in range(nb):
            hsum = hb_scr[g, pl.ds(b, tc, stride=nb), :]
            y_ref[b, :, lanes] = hsum * jax.nn.gelu(gate_ref[b, :, lanes])


def _lru_mixer(lru3, cw, cb, wg, ba, bx, nsp, tc):
    nb, seq, _ = lru3.shape
    assert nb == SUBLANES, "the RG-LRU kernels put the batch on the sublane axis"
    nblk = seq // tc
    rows = tc * nb
    hb = tc // HALO
    const2 = lambda i: (0, 0)
    const3 = lambda i: (0, 0, 0)
    xc_tm, hf_tm = pl.pallas_call(
        _lru_fwd_kernel,
        grid=(nblk,),
        in_specs=[pl.BlockSpec((nb, tc, LRU_WIDTH), lambda i: (0, i, 0)),
                  pl.BlockSpec((nb, HALO, LRU_WIDTH), lambda i: (0, jnp.maximum(i * hb - 1, 0), 0)),
                  pl.BlockSpec((nb, HALO, LRU_WIDTH),
                               lambda i: (0, jnp.minimum((i + 1) * hb, seq // HALO - 1), 0)),
                  pl.BlockSpec(cw.shape, const2), pl.BlockSpec(cb.shape, const2),
                  pl.BlockSpec(wg.shape[1:], const3), pl.BlockSpec(ba.shape[1:], const2),
                  pl.BlockSpec(bx.shape[1:], const2), pl.BlockSpec(nsp.shape[1:], const2)],
        out_specs=[pl.BlockSpec((rows, LRU_WIDTH), lambda i: (i, 0)),
                   pl.BlockSpec((rows, LRU_WIDTH), lambda i: (i, 0))],
        out_shape=[jax.ShapeDtypeStruct((seq * nb, LRU_WIDTH), F32),
                   jax.ShapeDtypeStruct((seq * nb, LRU_WIDTH), F32)],
        scratch_shapes=[pltpu.VMEM((LRU_GROUPS, (tc + 2 * HALO) * nb, LANES), F32),
                        pltpu.VMEM((rows, LRU_WIDTH), F32),
                        pltpu.VMEM((nb, LRU_WIDTH), F32)],
        compiler_params=_params("arbitrary"),
        name="lru_fwd",
    )(lru3, lru3, lru3, cw, cb, wg[0], ba[0], bx[0], nsp[0])

    rev = lambda i: (nblk - 1 - i, 0)
    return pl.pallas_call(
        _lru_bwd_kernel,
        grid=(nblk,),
        in_specs=[pl.BlockSpec((rows, LRU_WIDTH), rev),
                  pl.BlockSpec((rows, LRU_WIDTH), rev),
                  pl.BlockSpec((nb, tc, LRU_WIDTH), lambda i: (0, nblk - 1 - i, 1)),
                  pl.BlockSpec(wg.shape[1:], const3), pl.BlockSpec(ba.shape[1:], const2),
                  pl.BlockSpec(bx.shape[1:], const2), pl.BlockSpec(nsp.shape[1:], const2)],
        out_specs=pl.BlockSpec((nb, tc, LRU_WIDTH), lambda i: (0, nblk - 1 - i, 0)),
        out_shape=jax.ShapeDtypeStruct((nb, seq, LRU_WIDTH), F32),
        scratch_shapes=[pltpu.VMEM((rows, LRU_WIDTH), F32),
                        pltpu.VMEM((LRU_GROUPS, rows, LANES), F32),
                        pltpu.VMEM((LRU_GROUPS, nb, LANES), F32)],
        compiler_params=_params("arbitrary"),
        name="lru_bwd",
    )(xc_tm, hf_tm, lru3, wg[1], ba[1], bx[1], nsp[1])


def _lru_weights(w_a, b_a, w_x, b_x, lam):
    nblocks = w_a.shape[1]
    per_group = LANES // w_a.shape[2]

    def blockdiag(w):
        w = w.reshape(2, nblocks // per_group, per_group, w.shape[2], w.shape[3])
        eye = jnp.eye(per_group, dtype=w.dtype)
        return jnp.einsum('dgpij,pq->dgpiqj', w, eye).reshape(2, nblocks // per_group, LANES, LANES)

    wg = jnp.concatenate([blockdiag(w_a), blockdiag(w_x)], axis=-1).astype(BF16)
    nsp = -LRU_C * jax.nn.softplus(-lam.astype(F32))
    return wg, b_a.astype(F32)[:, None, :], b_x.astype(F32)[:, None, :], nsp[:, None, :]


def _ret_tables(seq):
    c = CHUNK
    log_g = jnp.log1p(-jnp.exp2(-5.0 - jnp.arange(RET_HEADS, dtype=F32)))
    lg_lane = jnp.repeat(log_g, HEAD_DIM)[None, :]
    idx = jnp.arange(c, dtype=F32)[:, None]
    t = {}
    t['q_f'] = jnp.exp((idx + 1.0) * lg_lane)
    t['q_b'] = jnp.exp((c - idx) * lg_lane)
    t['k_f'] = jnp.exp((c - 1 - idx) * lg_lane)
    t['k_b'] = jnp.exp(idx * lg_lane)
    dec = jnp.exp(jnp.abs(idx - idx.T)[None] * log_g[:, None, None])
    t['decay'] = dec.reshape(RET_GROUPS, 2, c, c).transpose(0, 2, 1, 3).reshape(RET_GROUPS, c, 2 * c)
    li = np.arange(LANES)
    same = jnp.asarray((li[:, None] // HEAD_DIM) == (li[None, :] // HEAD_DIM), F32)
    g_chunk = jnp.exp(c * log_g).reshape(RET_GROUPS, LANES // HEAD_DIM)
    t['gch'] = jnp.repeat(g_chunk, HEAD_DIM, axis=1)[:, :, None] * same[None]
    t['bd'] = same[None]
    half = HEAD_DIM // 2
    inv_freq = ROPE_BASE ** (-jnp.arange(half, dtype=F32) / half)
    ang = jnp.arange(seq, dtype=F32)[:, None] * inv_freq[None, :]
    cos, sin = jnp.cos(ang), jnp.sin(ang)
    reps = LANES // HEAD_DIM
    t['cos'] = jnp.tile(jnp.concatenate([cos, cos], axis=1), (1, reps))
    t['sin'] = jnp.tile(jnp.concatenate([-sin, sin], axis=1), (1, reps))
    return t


def _rope(t, cos, sin):
    half = HEAD_DIM // 2
    lane = lax.broadcasted_iota(jnp.int32, t.shape, 1)
    first = (lane % HEAD_DIM) < half
    partner = jnp.where(first, pltpu.roll(t, LANES - half, 1), pltpu.roll(t, half, 1))
    return t * cos + partner * sin


def _head_masks(shape):
    lane = lax.broadcasted_iota(jnp.int32, shape, 1)
    lo = (lane < HEAD_DIM).astype(F32)
    return lo, 1.0 - lo


def _ret_bwd_kernel(k_ref, v_ref, cos_ref, sin_ref, kb_ref, gch_ref, bd_ref, st_ref, s_scr):
    j = pl.program_id(1)

    @pl.when(j == 0)
    def _():
        s_scr[...] = jnp.zeros_like(s_scr)

    nchunk = k_ref.shape[0] // CHUNK
    for c in reversed(range(nchunk)):
        rows = slice(c * CHUNK, (c + 1) * CHUNK)
        for g in range(RET_GROUPS):
            lanes = slice(g * LANES, (g + 1) * LANES)
            st_ref[c, g] = s_scr[g].astype(st_ref.dtype)
            k = _rope(k_ref[rows, lanes], cos_ref[rows, :], sin_ref[rows, :]) * (HEAD_DIM ** -0.5)
            kv = _dot_tn((k * kb_ref[:, lanes]).astype(BF16), v_ref[rows, lanes].astype(BF16))
            s_scr[g] = gch_ref[g] * s_scr[g] + bd_ref[0] * kv


def _ret_fwd_kernel(q_ref, k_ref, v_ref, g_ref, cos_ref, sin_ref, qf_ref, qb_ref, kf_ref, dec_ref,
                    gch_ref, bd_ref, gnw_ref, stb_ref, y_ref, s_scr):
    j = pl.program_id(1)

    @pl.when(j == 0)
    def _():
        s_scr[...] = jnp.zeros_like(s_scr)

    nchunk = q_ref.shape[0] // CHUNK
    m_lo, m_hi = _head_masks((CHUNK, LANES))
    gn_avg = bd_ref[0].astype(BF16)

    def head_mean(t):
        hi = t.astype(BF16)
        lo = (t - hi.astype(F32)).astype(BF16)
        return (_dot(hi, gn_avg) + _dot(lo, gn_avg)) * (1.0 / HEAD_DIM)

    for c in range(nchunk):
        rows = slice(c * CHUNK, (c + 1) * CHUNK)
        cos = cos_ref[rows, :]
        sin = sin_ref[rows, :]
        for g in range(RET_GROUPS):
            lanes = slice(g * LANES, (g + 1) * LANES)
            q = _rope(q_ref[rows, lanes], cos, sin)
            k = _rope(k_ref[rows, lanes], cos, sin) * (HEAD_DIM ** -0.5)
            v = v_ref[rows, lanes]
            k2 = jnp.concatenate([k * m_lo, k * m_hi], axis=0).astype(BF16)
            s = _dot_nt(q.astype(BF16), k2) * dec_ref[g]
            v2 = jnp.concatenate([v * m_lo, v * m_hi], axis=0).astype(BF16)
            o = _dot(s.astype(BF16), v2)
            q2 = jnp.concatenate([(q * qf_ref[:, lanes]).astype(BF16),
                                  (q * qb_ref[:, lanes]).astype(BF16)], axis=1)
            st2 = jnp.concatenate([s_scr[g].astype(BF16), stb_ref[c, g].astype(BF16)], axis=0)
            o = o + _dot(q2, st2)
            kv = _dot_tn((k * kf_ref[:, lanes]).astype(BF16), v.astype(BF16))
            s_scr[g] = gch_ref[g] * s_scr[g] + bd_ref[0] * kv
            oc = o - head_mean(o)
            var = head_mean(oc * oc)
            o = oc * lax.rsqrt(var + GN_EPS) * gnw_ref[:, lanes]
            gt = g_ref[rows, lanes]
            y_ref[rows, lanes] = (gt * _sigmoid(gt)) * o


def _ret_mixer(ret3, tabs, gn_w, tb):
    nb, seq, _ = ret3.shape
    nblk = seq // tb
    cpb = tb // CHUNK
    w = RET_WIDTH
    const2 = lambda b, j: (0, 0)
    const3 = lambda b, j: (0, 0, 0)
    tab2 = lambda name: pl.BlockSpec(tabs[name].shape, const2)
    tab3 = lambda name: pl.BlockSpec(tabs[name].shape, const3)

    def col(cidx, rev):
        if rev:
            return pl.BlockSpec((None, tb, w), lambda b, j: (b, nblk - 1 - j, cidx))
        return pl.BlockSpec((None, tb, w), lambda b, j: (b, j, cidx))

    st_b = pl.pallas_call(
        _ret_bwd_kernel,
        grid=(nb, nblk),
        in_specs=[col(1, True), col(2, True),
                  pl.BlockSpec((tb, LANES), lambda b, j: (nblk - 1 - j, 0)),
                  pl.BlockSpec((tb, LANES), lambda b, j: (nblk - 1 - j, 0)),
                  tab2('k_b'), tab3('gch'), tab3('bd')],
        out_specs=pl.BlockSpec((None, cpb, RET_GROUPS, LANES, LANES),
                               lambda b, j: (b, nblk - 1 - j, 0, 0, 0)),
        out_shape=jax.ShapeDtypeStruct((nb, seq // CHUNK, RET_GROUPS, LANES, LANES), BF16),
        scratch_shapes=[pltpu.VMEM((RET_GROUPS, LANES, LANES), F32)],
        compiler_params=_params("parallel", "arbitrary"),
        name="ret_bwd",
    )(ret3, ret3, tabs['cos'], tabs['sin'], tabs['k_b'], tabs['gch'], tabs['bd'])

    return pl.pallas_call(
        _ret_fwd_kernel,
        grid=(nb, nblk),
        in_specs=[col(0, False), col(1, False), col(2, False), col(3, False),
                  pl.BlockSpec((tb, LANES), lambda b, j: (j, 0)),
                  pl.BlockSpec((tb, LANES), lambda b, j: (j, 0)),
                  tab2('q_f'), tab2('q_b'), tab2('k_f'), tab3('decay'), tab3('gch'), tab3('bd'),
                  pl.BlockSpec(gn_w.shape, const2),
                  pl.BlockSpec((None, cpb, RET_GROUPS, LANES, LANES), lambda b, j: (b, j, 0, 0, 0))],
        out_specs=pl.BlockSpec((None, tb, w), lambda b, j: (b, j, 0)),
        out_shape=jax.ShapeDtypeStruct((nb, seq, w), F32),
        scratch_shapes=[pltpu.VMEM((RET_GROUPS, LANES, LANES), F32)],
        compiler_params=_params("parallel", "arbitrary"),
        name="ret_fwd",
    )(ret3, ret3, ret3, ret3, tabs['cos'], tabs['sin'], tabs['q_f'], tabs['q_b'], tabs['k_f'],
      tabs['decay'], tabs['gch'], tabs['bd'], gn_w, st_b)


def _na_bias_table(rpb, rows_n):
    kh = min(NA_MAX_KH, rows_n)
    var = np.arange(kh)
    j = np.arange(kh)
    dr = j[None, :] - var[:, None]
    ri = dr + NA_MAX_KH - 1
    c = np.arange(GRID_W)
    cstart = np.clip(c - NA_KW // 2, 0, GRID_W - NA_KW)
    kc = np.arange(GRID_W)
    dc = kc[None, :] - c[:, None]
    ci = np.clip(dc, 1 - NA_KW, NA_KW - 1) + NA_KW - 1
    valid = (kc[None, :] >= cstart[:, None]) & (kc[None, :] < cstart[:, None] + NA_KW)
    b = rpb.astype(F32)[:, ri[:, None, :, None], ci[None, :, None, :]]
    b = jnp.where(valid[None, None, :, None, :], b, -jnp.inf)
    b = b.transpose(1, 0, 2, 3, 4)
    return b.reshape(kh, NA_HEADS * GRID_W, kh * GRID_W)


def _na_kernel(q_ref, k_ref, v_ref, bias_ref, y_ref, *, rows_per_step, rows_n):
    jb = pl.program_id(1)
    kh = min(NA_MAX_KH, rows_n)
    slab = kh * GRID_W
    lane = lax.broadcasted_iota(jnp.int32, (GRID_W, NA_WIDTH), 1)
    masks = [((lane // HEAD_DIM) == h).astype(F32) for h in range(NA_HEADS)]

    def row(rr, carry):
        r = jb * rows_per_step + rr
        rstart = jnp.clip(r - kh // 2, 0, rows_n - kh)
        q0 = pl.multiple_of(rr * GRID_W, GRID_W)
        k0 = pl.multiple_of(rstart * GRID_W, GRID_W)
        q = q_ref[pl.ds(q0, GRID_W), :].astype(F32) * (HEAD_DIM ** -0.5)
        q4 = jnp.concatenate([q * m for m in masks], axis=0).astype(BF16)
        s = _dot_nt(q4, k_ref[pl.ds(k0, slab), :]) + bias_ref[r - rstart]
        m = jnp.max(s, axis=1, keepdims=True)
        p = jnp.exp(s - m)
        l = jnp.sum(p, axis=1, keepdims=True)
        o4 = _dot(p.astype(BF16), v_ref[pl.ds(k0, slab), :]) / l
        o = o4[0:GRID_W] * masks[0]
        for h in range(1, NA_HEADS):
            o = o + o4[h * GRID_W:(h + 1) * GRID_W] * masks[h]
        y_ref[pl.ds(q0, GRID_W), :] = o.astype(y_ref.dtype)
        return carry

    lax.fori_loop(0, rows_per_step, row, 0)


def _na_mixer(na3, bias, rows_per_step):
    nb, seq, _ = na3.shape
    rows_n = seq // GRID_W
    tq = rows_per_step * GRID_W
    kern = functools.partial(_na_kernel, rows_per_step=rows_per_step, rows_n=rows_n)
    return pl.pallas_call(
        kern,
        grid=(nb, rows_n // rows_per_step),
        in_specs=[pl.BlockSpec((None, tq, NA_WIDTH), lambda b, j: (b, j, 0)),
                  pl.BlockSpec((None, seq, NA_WIDTH), lambda b, j: (b, 0, 1)),
                  pl.BlockSpec((None, seq, NA_WIDTH), lambda b, j: (b, 0, 2)),
                  pl.BlockSpec(bias.shape, lambda b, j: (0, 0, 0))],
        out_specs=pl.BlockSpec((None, tq, NA_WIDTH), lambda b, j: (b, j, 0)),
        out_shape=jax.ShapeDtypeStruct((nb, seq, NA_WIDTH), F32),
        compiler_params=_params("parallel", "arbitrary"),
        name="na",
    )(na3, na3, na3, bias)


def _layer_norm(z, g, b):
    mu = jnp.mean(z, axis=-1, keepdims=True)
    zc = z - mu
    var = jnp.mean(zc * zc, axis=-1, keepdims=True)
    return zc * lax.rsqrt(var + LN_EPS) * g + b


def _out_proj_kernel(x_ref, yl_ref, yr_ref, yn_ref, w_ref, g_ref, b_ref, o_ref, *, alpha):
    y = jnp.concatenate([yl_ref[...].astype(BF16), yr_ref[...].astype(BF16), yn_ref[...].astype(BF16)],
                        axis=1)
    z = alpha * x_ref[...] + _dot(y, w_ref[...])
    o_ref[...] = _layer_norm(z, g_ref[...], b_ref[...])


def _out_proj(x2, yl, yr, yn, w_bf, g, b, alpha, tm):
    n, d = x2.shape
    row = lambda width: pl.BlockSpec((tm, width), lambda i: (i, 0))
    const = lambda a: pl.BlockSpec(a.shape, lambda i: (0, 0))
    return pl.pallas_call(
        functools.partial(_out_proj_kernel, alpha=alpha),
        grid=(n // tm,),
        in_specs=[row(d), row(yl.shape[1]), row(yr.shape[1]), row(yn.shape[1]),
                  const(w_bf), const(g), const(b)],
        out_specs=row(d),
        out_shape=jax.ShapeDtypeStruct((n, d), F32),
        compiler_params=_params("parallel"),
        name="out_proj_ln",
    )(x2, yl, yr, yn, w_bf, g, b)


FFN_CHUNK = 256


def _ffn_kernel(x_ref, wg_ref, wu_ref, wd_ref, g_ref, b_ref, o_ref, acc_ref, *, alpha):
    x = x_ref[...]
    xb = x.astype(BF16)
    acc_ref[...] = alpha * x
    nchunk = wg_ref.shape[1] // FFN_CHUNK
    for c in range(nchunk):
        cols = slice(c * FFN_CHUNK, (c + 1) * FFN_CHUNK)
        gate = _dot(xb, wg_ref[:, cols])
        up = _dot(xb, wu_ref[:, cols])
        hid = (gate * _sigmoid(gate) * up).astype(BF16)
        acc_ref[...] += _dot(hid, wd_ref[cols, :])
    o_ref[...] = _layer_norm(acc_ref[...], g_ref[...], b_ref[...])


def _ffn(x2, wg_bf, wu_bf, wd_bf, g, b, alpha, tm):
    n, d = x2.shape
    row = pl.BlockSpec((tm, d), lambda i: (i, 0))
    const = lambda a: pl.BlockSpec(a.shape, lambda i: (0, 0))
    return pl.pallas_call(
        functools.partial(_ffn_kernel, alpha=alpha),
        grid=(n // tm,),
        in_specs=[row, const(wg_bf), const(wu_bf), const(wd_bf), const(g), const(b)],
        out_specs=row,
        out_shape=jax.ShapeDtypeStruct((n, d), F32),
        scratch_shapes=[pltpu.VMEM((tm, d), F32)],
        compiler_params=_params("parallel"),
        name="ffn_ln",
    )(x2, wg_bf, wu_bf, wd_bf, g, b)


ROW_TILE = 512
LRU_TIME_BLOCK = 256
RET_TIME_BLOCK = 512
NA_ROWS_PER_STEP = 8


def kernel(x, w_in, conv_w, conv_b, lru_w_a, lru_b_a, lru_w_x, lru_b_x, lru_lam, ret_gn_w, na_rpb,
           w_out, ln1_g, ln1_b, w_gate, w_up, w_down, ln2_g, ln2_b):
    nb, seq, d = x.shape
    depth = w_in.shape[0]
    alpha = (2 * depth) ** 0.25
    n = nb * seq
    tm = min(ROW_TILE, n)
    tabs = _ret_tables(seq)
    x2 = x.reshape(n, d).astype(F32)
    for l in range(depth):
        lru2, ret2, na2 = _in_proj(x2, w_in[l].astype(BF16), tm)
        wg, ba, bx, nsp = _lru_weights(lru_w_a[l], lru_b_a[l], lru_w_x[l], lru_b_x[l], lru_lam[l])
        y_lru = _lru_mixer(lru2.reshape(nb, seq, -1), conv_w[l].astype(F32), conv_b[l].astype(F32)[None, :],
                           wg, ba, bx, nsp, min(LRU_TIME_BLOCK, seq))
        y_ret = _ret_mixer(ret2.reshape(nb, seq, -1), tabs, ret_gn_w[l].astype(F32)[None, :],
                           min(RET_TIME_BLOCK, seq))
        bias = _na_bias_table(na_rpb[l], seq // GRID_W)
        y_na = _na_mixer(na2.reshape(nb, seq, -1), bias, NA_ROWS_PER_STEP)
        x2 = _out_proj(x2, y_lru.reshape(n, -1), y_ret.reshape(n, -1), y_na.reshape(n, -1),
                       w_out[l].astype(BF16), ln1_g[l].astype(F32)[None, :], ln1_b[l].astype(F32)[None, :],
                       alpha, tm)
        x2 = _ffn(x2, w_gate[l].astype(BF16), w_up[l].astype(BF16), w_down[l].astype(BF16),
                  ln2_g[l].astype(F32)[None, :], ln2_b[l].astype(F32)[None, :], alpha, tm)
    return x2.reshape(nb, seq, d).astype(x.dtype)
```

```python
import functools
import math

import numpy as np
import jax
import jax.numpy as jnp
from jax import lax
from jax.experimental import pallas as pl
from jax.experimental.pallas import tpu as pltpu

F32 = jnp.float32
BF16 = jnp.bfloat16

LANES = 128
SUBLANES = 8
VMEM_LIMIT_BYTES = 56 * 1024 * 1024

GRID_W = 64
HEAD_DIM = 64
LRU_WIDTH = 384
RET_HEADS = 6
RET_WIDTH = 384
NA_HEADS = 4
NA_WIDTH = 256
CONV_WIDTH = 4
LRU_C = 8.0
ROPE_BASE = 10000.0
GN_EPS = 1e-6
NA_MAX_KH = 8
NA_KW = 16
LN_EPS = 1e-5
CHUNK = 128
HALO = SUBLANES

LRU_GROUPS = LRU_WIDTH // LANES
RET_GROUPS = RET_WIDTH // LANES


def _dot(a, b):
    return jnp.dot(a, b, preferred_element_type=F32)


def _dot_nt(a, b):
    return lax.dot_general(a, b, (((1,), (1,)), ((), ())), preferred_element_type=F32)


def _dot_tn(a, b):
    return lax.dot_general(a, b, (((0,), (0,)), ((), ())), preferred_element_type=F32)


def _params(*sem):
    return pltpu.CompilerParams(dimension_semantics=sem, vmem_limit_bytes=VMEM_LIMIT_BYTES)


def _sigmoid(z):
    return 1.0 / (1.0 + jnp.exp(-z))


def _in_proj_kernel(x_ref, w_ref, lru_ref, ret_ref, na_ref):
    xb = x_ref[...].astype(BF16)
    n_lru = lru_ref.shape[1]
    n_ret = ret_ref.shape[1]
    lru_ref[...] = _dot(xb, w_ref[:, :n_lru])
    ret_ref[...] = _dot(xb, w_ref[:, n_lru:n_lru + n_ret])
    na_ref[...] = _dot(xb, w_ref[:, n_lru + n_ret:]).astype(na_ref.dtype)


def _in_proj(x2, w_bf, tm):
    n, d = x2.shape
    n_lru, n_ret, n_na = 2 * LRU_WIDTH, 4 * RET_WIDTH, 3 * NA_WIDTH
    return pl.pallas_call(
        _in_proj_kernel,
        grid=(n // tm,),
        in_specs=[pl.BlockSpec((tm, d), lambda i: (i, 0)),
                  pl.BlockSpec(w_bf.shape, lambda i: (0, 0))],
        out_specs=[pl.BlockSpec((tm, n_lru), lambda i: (i, 0)),
                   pl.BlockSpec((tm, n_ret), lambda i: (i, 0)),
                   pl.BlockSpec((tm, n_na), lambda i: (i, 0))],
        out_shape=[jax.ShapeDtypeStruct((n, n_lru), F32),
                   jax.ShapeDtypeStruct((n, n_ret), F32),
                   jax.ShapeDtypeStruct((n, n_na), BF16)],
        compiler_params=_params("parallel"),
        name="in_proj",
    )(x2, w_bf)


LRU_ROW_TILE = 256


def _lru_gates(xc, wg_ref, ba_ref, bx_ref, nsp_ref, g):
    z = _dot(xc.astype(BF16), wg_ref[g])
    r = _sigmoid(z[:, :LANES] + ba_ref[:, g * LANES:(g + 1) * LANES])
    i = _sigmoid(z[:, LANES:] + bx_ref[:, g * LANES:(g + 1) * LANES])
    log_a = r * nsp_ref[:, g * LANES:(g + 1) * LANES]
    a = jnp.exp(log_a)
    u = jnp.sqrt(1.0 - a * a) * (i * xc)
    return a, u


def _lru_fwd_kernel(x_ref, prev_ref, next_ref, cw_ref, cb_ref, wg_ref, ba_ref, bx_ref, nsp_ref,
                    xc_ref, hf_ref, xs_scr, a_scr, h_scr):
    i = pl.program_id(0)
    nblk = pl.num_programs(0)
    nb, tc, _ = x_ref.shape
    rows = tc * nb
    halo_rows = HALO * nb

    @pl.when(i == 0)
    def _():
        h_scr[...] = jnp.zeros_like(h_scr)

    has_prev = (i > 0).astype(F32)
    has_next = (i < nblk - 1).astype(F32)
    for g in range(LRU_GROUPS):
        lanes = slice(g * LANES, (g + 1) * LANES)
        for b in range(nb):
            xs_scr[g, pl.ds(b, HALO, stride=nb), :] = prev_ref[b, :, lanes] * has_prev
            xs_scr[g, pl.ds(halo_rows + b, tc, stride=nb), :] = x_ref[b, :, lanes]
            xs_scr[g, pl.ds(halo_rows + rows + b, HALO, stride=nb), :] = next_ref[b, :, lanes] * has_next

    left = CONV_WIDTH // 2
    for g in range(LRU_GROUPS):
        lanes = slice(g * LANES, (g + 1) * LANES)

        def tile(k, carry, g=g, lanes=lanes):
            r0 = pl.multiple_of(k * LRU_ROW_TILE, LRU_ROW_TILE)
            xc = jnp.broadcast_to(cb_ref[:, lanes], (LRU_ROW_TILE, LANES))
            for j in range(CONV_WIDTH):
                off = halo_rows + (j - left) * nb
                xc = xc + xs_scr[g, pl.ds(r0 + off, LRU_ROW_TILE), :] * cw_ref[j:j + 1, lanes]
            a, u = _lru_gates(xc, wg_ref, ba_ref, bx_ref, nsp_ref, g)
            xc_ref[pl.ds(r0, LRU_ROW_TILE), lanes] = xc
            a_scr[pl.ds(r0, LRU_ROW_TILE), lanes] = a
            hf_ref[pl.ds(r0, LRU_ROW_TILE), lanes] = u
            return carry

        lax.fori_loop(0, rows // LRU_ROW_TILE, tile, 0)

    def step(t, h):
        r0 = pl.multiple_of(t * nb, nb)
        h = a_scr[pl.ds(r0, nb), :] * h + hf_ref[pl.ds(r0, nb), :]
        hf_ref[pl.ds(r0, nb), :] = h
        return h

    h_scr[...] = lax.fori_loop(0, tc, step, h_scr[...], unroll=8)


def _lru_bwd_kernel(xc_ref, hf_ref, gate_ref, wg_ref, ba_ref, bx_ref, nsp_ref,
                    y_ref, a_scr, hb_scr, h_scr):
    i = pl.program_id(0)
    nb, tc, _ = gate_ref.shape
    rows = tc * nb

    @pl.when(i == 0)
    def _():
        h_scr[...] = jnp.zeros_like(h_scr)

    for g in range(LRU_GROUPS):
        lanes = slice(g * LANES, (g + 1) * LANES)

        def tile(k, carry, g=g, lanes=lanes):
            r0 = pl.multiple_of(k * LRU_ROW_TILE, LRU_ROW_TILE)
            xc = xc_ref[pl.ds(r0, LRU_ROW_TILE), lanes]
            a, u = _lru_gates(xc, wg_ref, ba_ref, bx_ref, nsp_ref, g)
            a_scr[pl.ds(r0, LRU_ROW_TILE), lanes] = a
            hb_scr[g, pl.ds(r0, LRU_ROW_TILE), :] = u
            return carry

        lax.fori_loop(0, rows // LRU_ROW_TILE, tile, 0)

    def step(s, h):
        r0 = pl.multiple_of((tc - 1 - s) * nb, nb)
        new = []
        for g in range(LRU_GROUPS):
            lanes = slice(g * LANES, (g + 1) * LANES)
            hg = a_scr[pl.ds(r0, nb), lanes] * h[g] + hb_scr[g, pl.ds(r0, nb), :]
            hb_scr[g, pl.ds(r0, nb), :] = hg + hf_ref[pl.ds(r0, nb), lanes]
            new.append(hg)
        return tuple(new)

    h0 = tuple(h_scr[g] for g in range(LRU_GROUPS))
    hl = lax.fori_loop(0, tc, step, h0, unroll=8)
    for g in range(LRU_GROUPS):
        h_scr[g] = hl[g]

    for g in range(LRU_GROUPS):
        lanes = slice(g * LANES, (g + 1) * LANES)
        for b in range(nb):
            hsum = hb_scr[g, pl.ds(b, tc, stride=nb), :]
            y_ref[b, :, lanes] = hsum * jax.nn.gelu(gate_ref[b, :, lanes])


def _lru_mixer(lru3, cw, cb, wg, ba, bx, nsp, tc):
    nb, seq, _ = lru3.shape
    assert nb == SUBLANES, "the RG-LRU kernels put the batch on the sublane axis"
    nblk = seq // tc
    rows = tc * nb
    hb = tc // HALO
    const2 = lambda i: (0, 0)
    const3 = lambda i: (0, 0, 0)
    xc_tm, hf_tm = pl.pallas_call(
        _lru_fwd_kernel,
        grid=(nblk,),
        in_specs=[pl.BlockSpec((nb, tc, LRU_WIDTH), lambda i: (0, i, 0)),
                  pl.BlockSpec((nb, HALO, LRU_WIDTH), lambda i: (0, jnp.maximum(i * hb - 1, 0), 0)),
                  pl.BlockSpec((nb, HALO, LRU_WIDTH),
                               lambda i: (0, jnp.minimum((i + 1) * hb, seq // HALO - 1), 0)),
                  pl.BlockSpec(cw.shape, const2), pl.BlockSpec(cb.shape, const2),
                  pl.BlockSpec(wg.shape[1:], const3), pl.BlockSpec(ba.shape[1:], const2),
                  pl.BlockSpec(bx.shape[1:], const2), pl.BlockSpec(nsp.shape[1:], const2)],
        out_specs=[pl.BlockSpec((rows, LRU_WIDTH), lambda i: (i, 0)),
                   pl.BlockSpec((rows, LRU_WIDTH), lambda i: (i, 0))],
        out_shape=[jax.ShapeDtypeStruct((seq * nb, LRU_WIDTH), F32),
                   jax.ShapeDtypeStruct((seq * nb, LRU_WIDTH), F32)],
        scratch_shapes=[pltpu.VMEM((LRU_GROUPS, (tc + 2 * HALO) * nb, LANES), F32),
                        pltpu.VMEM((rows, LRU_WIDTH), F32),
                        pltpu.VMEM((nb, LRU_WIDTH), F32)],
        compiler_params=_params("arbitrary"),
        name="lru_fwd",
    )(lru3, lru3, lru3, cw, cb, wg[0], ba[0], bx[0], nsp[0])

    rev = lambda i: (nblk - 1 - i, 0)
    return pl.pallas_call(
        _lru_bwd_kernel,
        grid=(nblk,),
        in_specs=[pl.BlockSpec((rows, LRU_WIDTH), rev),
                  pl.BlockSpec((rows, LRU_WIDTH), rev),
                  pl.BlockSpec((nb, tc, LRU_WIDTH), lambda i: (0, nblk - 1 - i, 1)),
                  pl.BlockSpec(wg.shape[1:], const3), pl.BlockSpec(ba.shape[1:], const2),
                  pl.BlockSpec(bx.shape[1:], const2), pl.BlockSpec(nsp.shape[1:], const2)],
        out_specs=pl.BlockSpec((nb, tc, LRU_WIDTH), lambda i: (0, nblk - 1 - i, 0)),
        out_shape=jax.ShapeDtypeStruct((nb, seq, LRU_WIDTH), F32),
        scratch_shapes=[pltpu.VMEM((rows, LRU_WIDTH), F32),
                        pltpu.VMEM((LRU_GROUPS, rows, LANES), F32),
                        pltpu.VMEM((LRU_GROUPS, nb, LANES), F32)],
        compiler_params=_params("arbitrary"),
        name="lru_bwd",
    )(xc_tm, hf_tm, lru3, wg[1], ba[1], bx[1], nsp[1])


def _lru_weights(w_a, b_a, w_x, b_x, lam):
    nblocks = w_a.shape[1]
    per_group = LANES // w_a.shape[2]

    def blockdiag(w):
        w = w.reshape(2, nblocks // per_group, per_group, w.shape[2], w.shape[3])
        eye = jnp.eye(per_group, dtype=w.dtype)
        return jnp.einsum('dgpij,pq->dgpiqj', w, eye).reshape(2, nblocks // per_group, LANES, LANES)

    wg = jnp.concatenate([blockdiag(w_a), blockdiag(w_x)], axis=-1).astype(BF16)
    nsp = -LRU_C * jax.nn.softplus(-lam.astype(F32))
    return wg, b_a.astype(F32)[:, None, :], b_x.astype(F32)[:, None, :], nsp[:, None, :]


def _ret_tables(seq):
    c = CHUNK
    log_g = jnp.log1p(-jnp.exp2(-5.0 - jnp.arange(RET_HEADS, dtype=F32)))
    lg_lane = jnp.repeat(log_g, HEAD_DIM)[None, :]
    idx = jnp.arange(c, dtype=F32)[:, None]
    t = {}
    t['q_f'] = jnp.exp((idx + 1.0) * lg_lane)
    t['q_b'] = jnp.exp((c - idx) * lg_lane)
    t['k_f'] = jnp.exp((c - 1 - idx) * lg_lane)
    t['k_b'] = jnp.exp(idx * lg_lane)
    dec = jnp.exp(jnp.abs(idx - idx.T)[None] * log_g[:, None, None])
    t['decay'] = dec.reshape(RET_GROUPS, 2, c, c).transpose(0, 2, 1, 3).reshape(RET_GROUPS, c, 2 * c)
    li = np.arange(LANES)
    same = jnp.asarray((li[:, None] // HEAD_DIM) == (li[None, :] // HEAD_DIM), F32)
    g_chunk = jnp.exp(c * log_g).reshape(RET_GROUPS, LANES // HEAD_DIM)
    t['gch'] = jnp.repeat(g_chunk, HEAD_DIM, axis=1)[:, :, None] * same[None]
    t['bd'] = same[None]
    half = HEAD_DIM // 2
    inv_freq = ROPE_BASE ** (-jnp.arange(half, dtype=F32) / half)
    ang = jnp.arange(seq, dtype=F32)[:, None] * inv_freq[None, :]
    cos, sin = jnp.cos(ang), jnp.sin(ang)
    reps = LANES // HEAD_DIM
    t['cos'] = jnp.tile(jnp.concatenate([cos, cos], axis=1), (1, reps))
    t['sin'] = jnp.tile(jnp.concatenate([-sin, sin], axis=1), (1, reps))
    return t


def _rope(t, cos, sin):
    half = HEAD_DIM // 2
    lane = lax.broadcasted_iota(jnp.int32, t.shape, 1)
    first = (lane % HEAD_DIM) < half
    partner = jnp.where(first, pltpu.roll(t, LANES - half, 1), pltpu.roll(t, half, 1))
    return t * cos + partner * sin


def _head_masks(shape):
    lane = lax.broadcasted_iota(jnp.int32, shape, 1)
    lo = (lane < HEAD_DIM).astype(F32)
    return lo, 1.0 - lo


def _ret_bwd_kernel(k_ref, v_ref, cos_ref, sin_ref, kb_ref, gch_ref, bd_ref, st_ref, s_scr):
    j = pl.program_id(1)

    @pl.when(j == 0)
    def _():
        s_scr[...] = jnp.zeros_like(s_scr)

    nchunk = k_ref.shape[0] // CHUNK
    for c in reversed(range(nchunk)):
        rows = slice(c * CHUNK, (c + 1) * CHUNK)
        for g in range(RET_GROUPS):
            lanes = slice(g * LANES, (g + 1) * LANES)
            st_ref[c, g] = s_scr[g].astype(st_ref.dtype)
            k = _rope(k_ref[rows, lanes], cos_ref[rows, :], sin_ref[rows, :]) * (HEAD_DIM ** -0.5)
            kv = _dot_tn((k * kb_ref[:, lanes]).astype(BF16), v_ref[rows, lanes].astype(BF16))
            s_scr[g] = gch_ref[g] * s_scr[g] + bd_ref[0] * kv


def _ret_fwd_kernel(q_ref, k_ref, v_ref, g_ref, cos_ref, sin_ref, qf_ref, qb_ref, kf_ref, dec_ref,
                    gch_ref, bd_ref, gnw_ref, stb_ref, y_ref, s_scr):
    j = pl.program_id(1)

    @pl.when(j == 0)
    def _():
        s_scr[...] = jnp.zeros_like(s_scr)

    nchunk = q_ref.shape[0] // CHUNK
    m_lo, m_hi = _head_masks((CHUNK, LANES))
    gn_avg = bd_ref[0].astype(BF16)

    def head_mean(t):
        return _dot(t.astype(BF16), gn_avg) * (1.0 / HEAD_DIM)

    for c in range(nchunk):
        rows = slice(c * CHUNK, (c + 1) * CHUNK)
        cos = cos_ref[rows, :]
        sin = sin_ref[rows, :]
        for g in range(RET_GROUPS):
            lanes = slice(g * LANES, (g + 1) * LANES)
            q = _rope(q_ref[rows, lanes], cos, sin)
            k = _rope(k_ref[rows, lanes], cos, sin) * (HEAD_DIM ** -0.5)
            v = v_ref[rows, lanes]
            k2 = jnp.concatenate([k * m_lo, k * m_hi], axis=0).astype(BF16)
            s = _dot_nt(q.astype(BF16), k2) * dec_ref[g]
            v2 = jnp.concatenate([v * m_lo, v * m_hi], axis=0).astype(BF16)
            o = _dot(s.astype(BF16), v2)
            q2 = jnp.concatenate([(q * qf_ref[:, lanes]).astype(BF16),
                                  (q * qb_ref[:, lanes]).astype(BF16)], axis=1)
            st2 = jnp.concatenate([s_scr[g].astype(BF16), stb_ref[c, g].astype(BF16)], axis=0)
            o = o + _dot(q2, st2)
            kv = _dot_tn((k * kf_ref[:, lanes]).astype(BF16), v.astype(BF16))
            s_scr[g] = gch_ref[g] * s_scr[g] + bd_ref[0] * kv
            oc = o - head_mean(o)
            var = head_mean(oc * oc)
            o = oc * lax.rsqrt(var + GN_EPS) * gnw_ref[:, lanes]
            gt = g_ref[rows, lanes]
            y_ref[rows, lanes] = (gt * _sigmoid(gt)) * o


def _ret_mixer(ret3, tabs, gn_w, tb):
    nb, seq, _ = ret3.shape
    nblk = seq // tb
    cpb = tb // CHUNK
    w = RET_WIDTH
    const2 = lambda b, j: (0, 0)
    const3 = lambda b, j: (0, 0, 0)
    tab2 = lambda name: pl.BlockSpec(tabs[name].shape, const2)
    tab3 = lambda name: pl.BlockSpec(tabs[name].shape, const3)

    def col(cidx, rev):
        if rev:
            return pl.BlockSpec((None, tb, w), lambda b, j: (b, nblk - 1 - j, cidx))
        return pl.BlockSpec((None, tb, w), lambda b, j: (b, j, cidx))

    st_b = pl.pallas_call(
        _ret_bwd_kernel,
        grid=(nb, nblk),
        in_specs=[col(1, True), col(2, True),
                  pl.BlockSpec((tb, LANES), lambda b, j: (nblk - 1 - j, 0)),
                  pl.BlockSpec((tb, LANES), lambda b, j: (nblk - 1 - j, 0)),
                  tab2('k_b'), tab3('gch'), tab3('bd')],
        out_specs=pl.BlockSpec((None, cpb, RET_GROUPS, LANES, LANES),
                               lambda b, j: (b, nblk - 1 - j, 0, 0, 0)),
        out_shape=jax.ShapeDtypeStruct((nb, seq // CHUNK, RET_GROUPS, LANES, LANES), BF16),
        scratch_shapes=[pltpu.VMEM((RET_GROUPS, LANES, LANES), F32)],
        compiler_params=_params("parallel", "arbitrary"),
        name="ret_bwd",
    )(ret3, ret3, tabs['cos'], tabs['sin'], tabs['k_b'], tabs['gch'], tabs['bd'])

    return pl.pallas_call(
        _ret_fwd_kernel,
        grid=(nb, nblk),
        in_specs=[col(0, False), col(1, False), col(2, False), col(3, False),
                  pl.BlockSpec((tb, LANES), lambda b, j: (j, 0)),
                  pl.BlockSpec((tb, LANES), lambda b, j: (j, 0)),
                  tab2('q_f'), tab2('q_b'), tab2('k_f'), tab3('decay'), tab3('gch'), tab3('bd'),
                  pl.BlockSpec(gn_w.shape, const2),
                  pl.BlockSpec((None, cpb, RET_GROUPS, LANES, LANES), lambda b, j: (b, j, 0, 0, 0))],
        out_specs=pl.BlockSpec((None, tb, w), lambda b, j: (b, j, 0)),
        out_shape=jax.ShapeDtypeStruct((nb, seq, w), F32),
        scratch_shapes=[pltpu.VMEM((RET_GROUPS, LANES, LANES), F32)],
        compiler_params=_params("parallel", "arbitrary"),
        name="ret_fwd",
    )(ret3, ret3, ret3, ret3, tabs['cos'], tabs['sin'], tabs['q_f'], tabs['q_b'], tabs['k_f'],
      tabs['decay'], tabs['gch'], tabs['bd'], gn_w, st_b)


def _na_bias_table(rpb, rows_n):
    kh = min(NA_MAX_KH, rows_n)
    var = np.arange(kh)
    j = np.arange(kh)
    dr = j[None, :] - var[:, None]
    ri = dr + NA_MAX_KH - 1
    c = np.arange(GRID_W)
    cstart = np.clip(c - NA_KW // 2, 0, GRID_W - NA_KW)
    kc = np.arange(GRID_W)
    dc = kc[None, :] - c[:, None]
    ci = np.clip(dc, 1 - NA_KW, NA_KW - 1) + NA_KW - 1
    valid = (kc[None, :] >= cstart[:, None]) & (kc[None, :] < cstart[:, None] + NA_KW)
    row_sel = np.zeros((kh, kh, rpb.shape[1]), np.float32)
    row_sel[var[:, None], j[None, :], ri] = 1.0
    col_sel = np.zeros((GRID_W, GRID_W, rpb.shape[2]), np.float32)
    col_sel[c[:, None], kc[None, :], ci] = 1.0
    b = jnp.einsum('vja,hab,qkb->vhqjk', row_sel, rpb.astype(F32), col_sel,
                   precision=lax.Precision.HIGHEST)
    b = jnp.where(valid[None, None, :, None, :], b, -jnp.inf)
    return b.reshape(kh, NA_HEADS * GRID_W, kh * GRID_W)


def _na_kernel(q_ref, k_ref, v_ref, bias_ref, y_ref, *, rows_per_step, rows_n):
    jb = pl.program_id(1)
    kh = min(NA_MAX_KH, rows_n)
    slab = kh * GRID_W
    lane = lax.broadcasted_iota(jnp.int32, (GRID_W, NA_WIDTH), 1)
    masks = [((lane // HEAD_DIM) == h).astype(F32) for h in range(NA_HEADS)]

    def row(rr, carry):
        r = jb * rows_per_step + rr
        rstart = jnp.clip(r - kh // 2, 0, rows_n - kh)
        q0 = pl.multiple_of(rr * GRID_W, GRID_W)
        k0 = pl.multiple_of(rstart * GRID_W, GRID_W)
        q = q_ref[pl.ds(q0, GRID_W), :].astype(F32) * (HEAD_DIM ** -0.5)
        q4 = jnp.concatenate([q * m for m in masks], axis=0).astype(BF16)
        s = _dot_nt(q4, k_ref[pl.ds(k0, slab), :]) + bias_ref[r - rstart]
        m = jnp.max(s, axis=1, keepdims=True)
        p = jnp.exp(s - m)
        l = jnp.sum(p, axis=1, keepdims=True)
        o4 = _dot(p.astype(BF16), v_ref[pl.ds(k0, slab), :]) / l
        o = o4[(NA_HEADS - 1) * GRID_W:]
        for h in reversed(range(NA_HEADS - 1)):
            o = jnp.where(lane < (h + 1) * HEAD_DIM, o4[h * GRID_W:(h + 1) * GRID_W], o)
        y_ref[pl.ds(q0, GRID_W), :] = o.astype(y_ref.dtype)
        return carry

    lax.fori_loop(0, rows_per_step, row, 0, unroll=4)


def _na_mixer(na3, bias, rows_per_step):
    nb, seq, _ = na3.shape
    rows_n = seq // GRID_W
    tq = rows_per_step * GRID_W
    kern = functools.partial(_na_kernel, rows_per_step=rows_per_step, rows_n=rows_n)
    return pl.pallas_call(
        kern,
        grid=(nb, rows_n // rows_per_step),
        in_specs=[pl.BlockSpec((None, tq, NA_WIDTH), lambda b, j: (b, j, 0)),
                  pl.BlockSpec((None, seq, NA_WIDTH), lambda b, j: (b, 0, 1)),
                  pl.BlockSpec((None, seq, NA_WIDTH), lambda b, j: (b, 0, 2)),
                  pl.BlockSpec(bias.shape, lambda b, j: (0, 0, 0))],
        out_specs=pl.BlockSpec((None, tq, NA_WIDTH), lambda b, j: (b, j, 0)),
        out_shape=jax.ShapeDtypeStruct((nb, seq, NA_WIDTH), F32),
        compiler_params=_params("parallel", "arbitrary"),
        name="na",
    )(na3, na3, na3, bias)


def _layer_norm(z, g, b):
    mu = jnp.mean(z, axis=-1, keepdims=True)
    zc = z - mu
    var = jnp.mean(zc * zc, axis=-1, keepdims=True)
    return zc * lax.rsqrt(var + LN_EPS) * g + b


FFN_CHUNK = 256


def _mix_ffn_kernel(x_ref, yl_ref, yr_ref, yn_ref, wo_ref, g1_ref, b1_ref, wg_ref, wu_ref, wd_ref,
                    g2_ref, b2_ref, o_ref, acc_ref, *, alpha):
    y = jnp.concatenate([yl_ref[...].astype(BF16), yr_ref[...].astype(BF16), yn_ref[...].astype(BF16)],
                        axis=1)
    x1 = _layer_norm(alpha * x_ref[...] + _dot(y, wo_ref[...]), g1_ref[...], b1_ref[...])
    xb = x1.astype(BF16)
    acc_ref[...] = alpha * x1
    nchunk = wg_ref.shape[1] // FFN_CHUNK
    for c in range(nchunk):
        cols = slice(c * FFN_CHUNK, (c + 1) * FFN_CHUNK)
        gate = _dot(xb, wg_ref[:, cols])
        up = _dot(xb, wu_ref[:, cols])
        hid = (gate * _sigmoid(gate) * up).astype(BF16)
        acc_ref[...] += _dot(hid, wd_ref[cols, :])
    o_ref[...] = _layer_norm(acc_ref[...], g2_ref[...], b2_ref[...])


def _mix_ffn(x2, yl, yr, yn, wo_bf, g1, b1, wg_bf, wu_bf, wd_bf, g2, b2, alpha, tm):
    n, d = x2.shape
    row = lambda width: pl.BlockSpec((tm, width), lambda i: (i, 0))
    const = lambda a: pl.BlockSpec(a.shape, lambda i: (0, 0), pipeline_mode=pl.Buffered(1))
    return pl.pallas_call(
        functools.partial(_mix_ffn_kernel, alpha=alpha),
        grid=(n // tm,),
        in_specs=[row(d), row(yl.shape[1]), row(yr.shape[1]), row(yn.shape[1]),
                  const(wo_bf), const(g1), const(b1), const(wg_bf), const(wu_bf), const(wd_bf),
                  const(g2), const(b2)],
        out_specs=row(d),
        out_shape=jax.ShapeDtypeStruct((n, d), F32),
        scratch_shapes=[pltpu.VMEM((tm, d), F32)],
        compiler_params=_params("parallel"),
        name="mix_ffn_ln",
    )(x2, yl, yr, yn, wo_bf, g1, b1, wg_bf, wu_bf, wd_bf, g2, b2)


ROW_TILE = 512
LRU_TIME_BLOCK = 256
RET_TIME_BLOCK = 512
NA_ROWS_PER_STEP = 8


def kernel(x, w_in, conv_w, conv_b, lru_w_a, lru_b_a, lru_w_x, lru_b_x, lru_lam, ret_gn_w, na_rpb,
           w_out, ln1_g, ln1_b, w_gate, w_up, w_down, ln2_g, ln2_b):
    nb, seq, d = x.shape
    depth = w_in.shape[0]
    alpha = (2 * depth) ** 0.25
    n = nb * seq
    tm = min(ROW_TILE, n)
    tabs = _ret_tables(seq)
    x2 = x.reshape(n, d).astype(F32)
    for l in range(depth):
        lru2, ret2, na2 = _in_proj(x2, w_in[l].astype(BF16), tm)
        wg, ba, bx, nsp = _lru_weights(lru_w_a[l], lru_b_a[l], lru_w_x[l], lru_b_x[l], lru_lam[l])
        y_lru = _lru_mixer(lru2.reshape(nb, seq, -1), conv_w[l].astype(F32), conv_b[l].astype(F32)[None, :],
                           wg, ba, bx, nsp, min(LRU_TIME_BLOCK, seq))
        y_ret = _ret_mixer(ret2.reshape(nb, seq, -1), tabs, ret_gn_w[l].astype(F32)[None, :],
                           min(RET_TIME_BLOCK, seq))
        bias = _na_bias_table(na_rpb[l], seq // GRID_W)
        y_na = _na_mixer(na2.reshape(nb, seq, -1), bias, NA_ROWS_PER_STEP)
        vec = lambda a: a.astype(F32)[None, :]
        x2 = _mix_ffn(x2, y_lru.reshape(n, -1), y_ret.reshape(n, -1), y_na.reshape(n, -1),
                      w_out[l].astype(BF16), vec(ln1_g[l]), vec(ln1_b[l]),
                      w_gate[l].astype(BF16), w_up[l].astype(BF16), w_down[l].astype(BF16),
                      vec(ln2_g[l]), vec(ln2_b[l]), alpha, tm)
    return x2.reshape(nb, seq, d).astype(x.dtype)
```

```python
import functools
import math

import numpy as np
import jax
import jax.numpy as jnp
from jax import lax
from jax.experimental import pallas as pl
from jax.experimental.pallas import tpu as pltpu

F32 = jnp.float32
BF16 = jnp.bfloat16

LANES = 128
SUBLANES = 8
VMEM_LIMIT_BYTES = 56 * 1024 * 1024

GRID_W = 64
HEAD_DIM = 64
LRU_WIDTH = 384
RET_HEADS = 6
RET_WIDTH = 384
NA_HEADS = 4
NA_WIDTH = 256
CONV_WIDTH = 4
LRU_C = 8.0
ROPE_BASE = 10000.0
GN_EPS = 1e-6
NA_MAX_KH = 8
NA_KW = 16
LN_EPS = 1e-5
CHUNK = 128
HALO = SUBLANES

LRU_GROUPS = LRU_WIDTH // LANES
RET_GROUPS = RET_WIDTH // LANES


def _dot(a, b):
    return jnp.dot(a, b, preferred_element_type=F32)


def _dot_nt(a, b):
    return lax.dot_general(a, b, (((1,), (1,)), ((), ())), preferred_element_type=F32)


def _dot_tn(a, b):
    return lax.dot_general(a, b, (((0,), (0,)), ((), ())), preferred_element_type=F32)


def _params(*sem):
    return pltpu.CompilerParams(dimension_semantics=sem, vmem_limit_bytes=VMEM_LIMIT_BYTES)


def _sigmoid(z):
    return 1.0 / (1.0 + jnp.exp(-z))


def _in_proj_kernel(x_ref, w_ref, cos_ref, sin_ref, lx_ref, lg_ref, rqkv_ref, rg_ref, na_ref):
    xb = x_ref[...].astype(BF16)
    n_lru = 2 * LRU_WIDTH
    w = RET_WIDTH
    lru = _dot(xb, w_ref[:, :n_lru])
    lx_ref[...] = lru[:, :LRU_WIDTH]
    lg_ref[...] = jax.nn.gelu(lru[:, LRU_WIDTH:]).astype(lg_ref.dtype)
    ret = _dot(xb, w_ref[:, n_lru:n_lru + 4 * w])
    cos = cos_ref[...]
    sin = sin_ref[...]
    for g in range(2 * RET_GROUPS):
        lanes = slice(g * LANES, (g + 1) * LANES)
        t = _rope(ret[:, lanes], cos, sin)
        if g >= RET_GROUPS:
            t = t * (HEAD_DIM ** -0.5)
        rqkv_ref[:, lanes] = t.astype(rqkv_ref.dtype)
    rqkv_ref[:, 2 * w:] = ret[:, 2 * w:3 * w].astype(rqkv_ref.dtype)
    gate = ret[:, 3 * w:]
    rg_ref[...] = (gate * _sigmoid(gate)).astype(rg_ref.dtype)
    na_ref[...] = _dot(xb, w_ref[:, n_lru + 4 * w:]).astype(na_ref.dtype)


def _in_proj(x2, w_bf, cos, sin, tm):
    n, d = x2.shape
    seq_blocks = cos.shape[0] // tm
    n_na = 3 * NA_WIDTH
    row = lambda width: pl.BlockSpec((tm, width), lambda i: (i, 0))
    pos = pl.BlockSpec((tm, LANES), lambda i: (i % seq_blocks, 0))
    return pl.pallas_call(
        _in_proj_kernel,
        grid=(n // tm,),
        in_specs=[row(d), pl.BlockSpec(w_bf.shape, lambda i: (0, 0), pipeline_mode=pl.Buffered(1)),
                  pos, pos],
        out_specs=[row(LRU_WIDTH), row(LRU_WIDTH), row(3 * RET_WIDTH), row(RET_WIDTH), row(n_na)],
        out_shape=[jax.ShapeDtypeStruct((n, LRU_WIDTH), F32),
                   jax.ShapeDtypeStruct((n, LRU_WIDTH), BF16),
                   jax.ShapeDtypeStruct((n, 3 * RET_WIDTH), BF16),
                   jax.ShapeDtypeStruct((n, RET_WIDTH), BF16),
                   jax.ShapeDtypeStruct((n, n_na), BF16)],
        compiler_params=_params("parallel"),
        name="in_proj",
    )(x2, w_bf, cos, sin)


LRU_ROW_TILE = 256


def _lru_gates(xc, wg_ref, ba_ref, bx_ref, nsp_ref, g):
    z = _dot(xc.astype(BF16), wg_ref[g])
    r = _sigmoid(z[:, :LANES] + ba_ref[:, g * LANES:(g + 1) * LANES])
    i = _sigmoid(z[:, LANES:] + bx_ref[:, g * LANES:(g + 1) * LANES])
    log_a = r * nsp_ref[:, g * LANES:(g + 1) * LANES]
    a = jnp.exp(log_a)
    u = jnp.sqrt(1.0 - a * a) * (i * xc)
    return a, u


def _lru_fwd_kernel(x_ref, prev_ref, next_ref, cw_ref, cb_ref, wg_ref, ba_ref, bx_ref, nsp_ref,
                    xc_ref, hf_ref, xs_scr, a_scr, h_scr):
    i = pl.program_id(0)
    nblk = pl.num_programs(0)
    nb, tc, _ = x_ref.shape
    rows = tc * nb
    halo_rows = HALO * nb

    @pl.when(i == 0)
    def _():
        h_scr[...] = jnp.zeros_like(h_scr)

    has_prev = (i > 0).astype(F32)
    has_next = (i < nblk - 1).astype(F32)
    for g in range(LRU_GROUPS):
        lanes = slice(g * LANES, (g + 1) * LANES)
        for b in range(nb):
            xs_scr[g, pl.ds(b, HALO, stride=nb), :] = prev_ref[b, :, lanes] * has_prev
            xs_scr[g, pl.ds(halo_rows + b, tc, stride=nb), :] = x_ref[b, :, lanes]
            xs_scr[g, pl.ds(halo_rows + rows + b, HALO, stride=nb), :] = next_ref[b, :, lanes] * has_next

    left = CONV_WIDTH // 2
    for g in range(LRU_GROUPS):
        lanes = slice(g * LANES, (g + 1) * LANES)

        def tile(k, carry, g=g, lanes=lanes):
            r0 = pl.multiple_of(k * LRU_ROW_TILE, LRU_ROW_TILE)
            xc = jnp.broadcast_to(cb_ref[:, lanes], (LRU_ROW_TILE, LANES))
            for j in range(CONV_WIDTH):
                off = halo_rows + (j - left) * nb
                xc = xc + xs_scr[g, pl.ds(r0 + off, LRU_ROW_TILE), :] * cw_ref[j:j + 1, lanes]
            a, u = _lru_gates(xc, wg_ref, ba_ref, bx_ref, nsp_ref, g)
            xc_ref[pl.ds(r0, LRU_ROW_TILE), lanes] = xc
            a_scr[pl.ds(r0, LRU_ROW_TILE), lanes] = a
            hf_ref[pl.ds(r0, LRU_ROW_TILE), lanes] = u
            return carry

        lax.fori_loop(0, rows // LRU_ROW_TILE, tile, 0, unroll=2)

    def step(t, h):
        r0 = pl.multiple_of(t * nb, nb)
        h = a_scr[pl.ds(r0, nb), :] * h + hf_ref[pl.ds(r0, nb), :]
        hf_ref[pl.ds(r0, nb), :] = h
        return h

    h_scr[...] = lax.fori_loop(0, tc, step, h_scr[...], unroll=8)


def _lru_bwd_kernel(xc_ref, hf_ref, gate_ref, wg_ref, ba_ref, bx_ref, nsp_ref,
                    y_ref, a_scr, hb_scr, h_scr):
    i = pl.program_id(0)
    nb, tc, _ = gate_ref.shape
    rows = tc * nb

    @pl.when(i == 0)
    def _():
        h_scr[...] = jnp.zeros_like(h_scr)

    for g in range(LRU_GROUPS):
        lanes = slice(g * LANES, (g + 1) * LANES)

        def tile(k, carry, g=g, lanes=lanes):
            r0 = pl.multiple_of(k * LRU_ROW_TILE, LRU_ROW_TILE)
            xc = xc_ref[pl.ds(r0, LRU_ROW_TILE), lanes]
            a, u = _lru_gates(xc, wg_ref, ba_ref, bx_ref, nsp_ref, g)
            a_scr[pl.ds(r0, LRU_ROW_TILE), lanes] = a
            hb_scr[g, pl.ds(r0, LRU_ROW_TILE), :] = u
            return carry

        lax.fori_loop(0, rows // LRU_ROW_TILE, tile, 0, unroll=2)

    def step(s, h):
        r0 = pl.multiple_of((tc - 1 - s) * nb, nb)
        new = []
        for g in range(LRU_GROUPS):
            lanes = slice(g * LANES, (g + 1) * LANES)
            hg = a_scr[pl.ds(r0, nb), lanes] * h[g] + hb_scr[g, pl.ds(r0, nb), :]
            hb_scr[g, pl.ds(r0, nb), :] = hg + hf_ref[pl.ds(r0, nb), lanes]
            new.append(hg)
        return tuple(new)

    h0 = tuple(h_scr[g] for g in range(LRU_GROUPS))
    hl = lax.fori_loop(0, tc, step, h0, unroll=8)
    for g in range(LRU_GROUPS):
        h_scr[g] = hl[g]

    for g in range(LRU_GROUPS):
        lanes = slice(g * LANES, (g + 1) * LANES)
        for b in range(nb):
            hsum = hb_scr[g, pl.ds(b, tc, stride=nb), :]
            y_ref[b, :, lanes] = (hsum * gate_ref[b, :, lanes].astype(F32)).astype(y_ref.dtype)


def _lru_mixer(lx3, lg3, cw, cb, wg, ba, bx, nsp, tc):
    nb, seq, _ = lx3.shape
    assert nb == SUBLANES, "the RG-LRU kernels put the batch on the sublane axis"
    nblk = seq // tc
    rows = tc * nb
    hb = tc // HALO
    const2 = lambda i: (0, 0)
    const3 = lambda i: (0, 0, 0)
    xc_tm, hf_tm = pl.pallas_call(
        _lru_fwd_kernel,
        grid=(nblk,),
        in_specs=[pl.BlockSpec((nb, tc, LRU_WIDTH), lambda i: (0, i, 0)),
                  pl.BlockSpec((nb, HALO, LRU_WIDTH), lambda i: (0, jnp.maximum(i * hb - 1, 0), 0)),
                  pl.BlockSpec((nb, HALO, LRU_WIDTH),
                               lambda i: (0, jnp.minimum((i + 1) * hb, seq // HALO - 1), 0)),
                  pl.BlockSpec(cw.shape, const2), pl.BlockSpec(cb.shape, const2),
                  pl.BlockSpec(wg.shape[1:], const3), pl.BlockSpec(ba.shape[1:], const2),
                  pl.BlockSpec(bx.shape[1:], const2), pl.BlockSpec(nsp.shape[1:], const2)],
        out_specs=[pl.BlockSpec((rows, LRU_WIDTH), lambda i: (i, 0)),
                   pl.BlockSpec((rows, LRU_WIDTH), lambda i: (i, 0))],
        out_shape=[jax.ShapeDtypeStruct((seq * nb, LRU_WIDTH), F32),
                   jax.ShapeDtypeStruct((seq * nb, LRU_WIDTH), F32)],
        scratch_shapes=[pltpu.VMEM((LRU_GROUPS, (tc + 2 * HALO) * nb, LANES), F32),
                        pltpu.VMEM((rows, LRU_WIDTH), F32),
                        pltpu.VMEM((nb, LRU_WIDTH), F32)],
        compiler_params=_params("arbitrary"),
        name="lru_fwd",
    )(lx3, lx3, lx3, cw, cb, wg[0], ba[0], bx[0], nsp[0])

    rev = lambda i: (nblk - 1 - i, 0)
    return pl.pallas_call(
        _lru_bwd_kernel,
        grid=(nblk,),
        in_specs=[pl.BlockSpec((rows, LRU_WIDTH), rev),
                  pl.BlockSpec((rows, LRU_WIDTH), rev),
                  pl.BlockSpec((nb, tc, LRU_WIDTH), lambda i: (0, nblk - 1 - i, 0)),
                  pl.BlockSpec(wg.shape[1:], const3), pl.BlockSpec(ba.shape[1:], const2),
                  pl.BlockSpec(bx.shape[1:], const2), pl.BlockSpec(nsp.shape[1:], const2)],
        out_specs=pl.BlockSpec((nb, tc, LRU_WIDTH), lambda i: (0, nblk - 1 - i, 0)),
        out_shape=jax.ShapeDtypeStruct((nb, seq, LRU_WIDTH), BF16),
        scratch_shapes=[pltpu.VMEM((rows, LRU_WIDTH), F32),
                        pltpu.VMEM((LRU_GROUPS, rows, LANES), F32),
                        pltpu.VMEM((LRU_GROUPS, nb, LANES), F32)],
        compiler_params=_params("arbitrary"),
        name="lru_bwd",
    )(xc_tm, hf_tm, lg3, wg[1], ba[1], bx[1], nsp[1])


def _lru_weights(w_a, b_a, w_x, b_x, lam):
    nblocks = w_a.shape[1]
    per_group = LANES // w_a.shape[2]

    def blockdiag(w):
        w = w.reshape(2, nblocks // per_group, per_group, w.shape[2], w.shape[3])
        eye = jnp.eye(per_group, dtype=w.dtype)
        return jnp.einsum('dgpij,pq->dgpiqj', w, eye).reshape(2, nblocks // per_group, LANES, LANES)

    wg = jnp.concatenate([blockdiag(w_a), blockdiag(w_x)], axis=-1).astype(BF16)
    nsp = -LRU_C * jax.nn.softplus(-lam.astype(F32))
    return wg, b_a.astype(F32)[:, None, :], b_x.astype(F32)[:, None, :], nsp[:, None, :]


def _ret_tables(seq):
    c = CHUNK
    log_g = jnp.log1p(-jnp.exp2(-5.0 - jnp.arange(RET_HEADS, dtype=F32)))
    lg_lane = jnp.repeat(log_g, HEAD_DIM)[None, :]
    idx = jnp.arange(c, dtype=F32)[:, None]
    t = {}
    t['q_f'] = jnp.exp((idx + 1.0) * lg_lane)
    t['q_b'] = jnp.exp((c - idx) * lg_lane)
    t['k_f'] = jnp.exp((c - 1 - idx) * lg_lane)
    t['k_b'] = jnp.exp(idx * lg_lane)
    dec = jnp.exp(jnp.abs(idx - idx.T)[None] * log_g[:, None, None])
    t['decay'] = dec.reshape(RET_GROUPS, 2, c, c).transpose(0, 2, 1, 3).reshape(RET_GROUPS, c, 2 * c)
    li = np.arange(LANES)
    same = jnp.asarray((li[:, None] // HEAD_DIM) == (li[None, :] // HEAD_DIM), F32)
    g_chunk = jnp.exp(c * log_g).reshape(RET_GROUPS, LANES // HEAD_DIM)
    t['gch'] = jnp.repeat(g_chunk, HEAD_DIM, axis=1)[:, :, None] * same[None]
    t['bd'] = same[None]
    half = HEAD_DIM // 2
    inv_freq = ROPE_BASE ** (-jnp.arange(half, dtype=F32) / half)
    ang = jnp.arange(seq, dtype=F32)[:, None] * inv_freq[None, :]
    cos, sin = jnp.cos(ang), jnp.sin(ang)
    reps = LANES // HEAD_DIM
    t['cos'] = jnp.tile(jnp.concatenate([cos, cos], axis=1), (1, reps))
    t['sin'] = jnp.tile(jnp.concatenate([-sin, sin], axis=1), (1, reps))
    return t


def _rope(t, cos, sin):
    half = HEAD_DIM // 2
    lane = lax.broadcasted_iota(jnp.int32, t.shape, 1)
    first = (lane % HEAD_DIM) < half
    partner = jnp.where(first, pltpu.roll(t, LANES - half, 1), pltpu.roll(t, half, 1))
    return t * cos + partner * sin


def _split_heads(t, lo):
    zero = jnp.zeros_like(t)
    return jnp.concatenate([jnp.where(lo, t, zero), jnp.where(lo, zero, t)], axis=0)


def _ret_blk(ref, c, g):
    return ref[c * CHUNK:(c + 1) * CHUNK, g * LANES:(g + 1) * LANES]


def _ret_bwd_kernel(k_ref, v_ref, kb_ref, gch_ref, bd_ref, st_ref, s_scr):
    j = pl.program_id(1)

    @pl.when(j == 0)
    def _():
        s_scr[...] = jnp.zeros_like(s_scr)

    nchunk = k_ref.shape[0] // CHUNK
    kv = {}
    for c in range(nchunk):
        for g in range(RET_GROUPS):
            kb = kb_ref[:, g * LANES:(g + 1) * LANES]
            kv[c, g] = _dot_tn((_ret_blk(k_ref, c, g).astype(F32) * kb).astype(BF16), _ret_blk(v_ref, c, g))
    for g in range(RET_GROUPS):
        st = s_scr[g]
        for c in reversed(range(nchunk)):
            st_ref[c, g] = st.astype(st_ref.dtype)
            st = gch_ref[g] * st + bd_ref[0] * kv[c, g]
        s_scr[g] = st


def _ret_fwd_kernel(q_ref, k_ref, v_ref, g_ref, qf_ref, qb_ref, kf_ref, dec_ref,
                    gch_ref, bd_ref, gnw_ref, stb_ref, y_ref, s_scr):
    j = pl.program_id(1)

    @pl.when(j == 0)
    def _():
        s_scr[...] = jnp.zeros_like(s_scr)

    nchunk = q_ref.shape[0] // CHUNK
    units = [(c, g) for c in range(nchunk) for g in range(RET_GROUPS)]
    lo = lax.broadcasted_iota(jnp.int32, (CHUNK, LANES), 1) < HEAD_DIM
    gn_avg = bd_ref[0].astype(BF16)

    s, kv = {}, {}
    for c, g in units:
        lanes = slice(g * LANES, (g + 1) * LANES)
        k = _ret_blk(k_ref, c, g)
        s[c, g] = _dot_nt(_ret_blk(q_ref, c, g), _split_heads(k.astype(F32), lo).astype(BF16))
        kv[c, g] = _dot_tn((k.astype(F32) * kf_ref[:, lanes]).astype(BF16), _ret_blk(v_ref, c, g))
    o = {}
    for c, g in units:
        v2 = _split_heads(_ret_blk(v_ref, c, g).astype(F32), lo).astype(BF16)
        o[c, g] = _dot((s[c, g] * dec_ref[g]).astype(BF16), v2)
    for g in range(RET_GROUPS):
        lanes = slice(g * LANES, (g + 1) * LANES)
        st = s_scr[g]
        for c in range(nchunk):
            q = _ret_blk(q_ref, c, g).astype(F32)
            q2 = jnp.concatenate([(q * qf_ref[:, lanes]).astype(BF16), (q * qb_ref[:, lanes]).astype(BF16)],
                                 axis=1)
            st2 = jnp.concatenate([st.astype(BF16), stb_ref[c, g]], axis=0)
            o[c, g] = o[c, g] + _dot(q2, st2)
            st = gch_ref[g] * st + bd_ref[0] * kv[c, g]
        s_scr[g] = st
    for g in range(RET_GROUPS):
        lanes = slice(g * LANES, (g + 1) * LANES)
        og = jnp.concatenate([o[c, g] for c in range(nchunk)], axis=0)
        oc = og - _dot(og.astype(BF16), gn_avg) * (1.0 / HEAD_DIM)
        var = _dot((oc * oc).astype(BF16), gn_avg) * (1.0 / HEAD_DIM)
        y = g_ref[:, lanes].astype(F32) * (oc * lax.rsqrt(var + GN_EPS) * gnw_ref[:, lanes])
        y_ref[:, lanes] = y.astype(y_ref.dtype)


def _ret_mixer(rqkv3, rg3, tabs, gn_w, tb):
    nb, seq, _ = rqkv3.shape
    nblk = seq // tb
    cpb = tb // CHUNK
    w = RET_WIDTH
    const2 = lambda b, j: (0, 0)
    const3 = lambda b, j: (0, 0, 0)
    tab2 = lambda name: pl.BlockSpec(tabs[name].shape, const2)
    tab3 = lambda name: pl.BlockSpec(tabs[name].shape, const3)

    def col(cidx, rev):
        if rev:
            return pl.BlockSpec((None, tb, w), lambda b, j: (b, nblk - 1 - j, cidx))
        return pl.BlockSpec((None, tb, w), lambda b, j: (b, j, cidx))

    st_b = pl.pallas_call(
        _ret_bwd_kernel,
        grid=(nb, nblk),
        in_specs=[col(1, True), col(2, True), tab2('k_b'), tab3('gch'), tab3('bd')],
        out_specs=pl.BlockSpec((None, cpb, RET_GROUPS, LANES, LANES),
                               lambda b, j: (b, nblk - 1 - j, 0, 0, 0)),
        out_shape=jax.ShapeDtypeStruct((nb, seq // CHUNK, RET_GROUPS, LANES, LANES), BF16),
        scratch_shapes=[pltpu.VMEM((RET_GROUPS, LANES, LANES), F32)],
        compiler_params=_params("parallel", "arbitrary"),
        name="ret_bwd",
    )(rqkv3, rqkv3, tabs['k_b'], tabs['gch'], tabs['bd'])

    return pl.pallas_call(
        _ret_fwd_kernel,
        grid=(nb, nblk),
        in_specs=[col(0, False), col(1, False), col(2, False), col(0, False),
                  tab2('q_f'), tab2('q_b'), tab2('k_f'), tab3('decay'), tab3('gch'), tab3('bd'),
                  pl.BlockSpec(gn_w.shape, const2),
                  pl.BlockSpec((None, cpb, RET_GROUPS, LANES, LANES), lambda b, j: (b, j, 0, 0, 0))],
        out_specs=pl.BlockSpec((None, tb, w), lambda b, j: (b, j, 0)),
        out_shape=jax.ShapeDtypeStruct((nb, seq, w), BF16),
        scratch_shapes=[pltpu.VMEM((RET_GROUPS, LANES, LANES), F32)],
        compiler_params=_params("parallel", "arbitrary"),
        name="ret_fwd",
    )(rqkv3, rqkv3, rqkv3, rg3, tabs['q_f'], tabs['q_b'], tabs['k_f'],
      tabs['decay'], tabs['gch'], tabs['bd'], gn_w, st_b)


def _na_bias_table(rpb, rows_n):
    kh = min(NA_MAX_KH, rows_n)
    var = np.arange(kh)
    j = np.arange(kh)
    dr = j[None, :] - var[:, None]
    ri = dr + NA_MAX_KH - 1
    c = np.arange(GRID_W)
    cstart = np.clip(c - NA_KW // 2, 0, GRID_W - NA_KW)
    kc = np.arange(GRID_W)
    dc = kc[None, :] - c[:, None]
    ci = np.clip(dc, 1 - NA_KW, NA_KW - 1) + NA_KW - 1
    valid = (kc[None, :] >= cstart[:, None]) & (kc[None, :] < cstart[:, None] + NA_KW)
    row_sel = np.zeros((kh, kh, rpb.shape[1]), np.float32)
    row_sel[var[:, None], j[None, :], ri] = 1.0
    col_sel = np.zeros((GRID_W, GRID_W, rpb.shape[2]), np.float32)
    col_sel[c[:, None], kc[None, :], ci] = 1.0
    b = jnp.einsum('vja,hab,qkb->vhqjk', row_sel, rpb.astype(F32), col_sel,
                   precision=lax.Precision.HIGHEST)
    b = jnp.where(valid[None, None, :, None, :], b, -jnp.inf)
    return b.reshape(kh, NA_HEADS * GRID_W, kh * GRID_W)


def _na_kernel(q_ref, k_ref, v_ref, bias_ref, y_ref, *, rows_per_step, rows_n):
    jb = pl.program_id(1)
    kh = min(NA_MAX_KH, rows_n)
    slab = kh * GRID_W
    lane = lax.broadcasted_iota(jnp.int32, (GRID_W, NA_WIDTH), 1)
    zero = jnp.zeros((GRID_W, NA_WIDTH), q_ref.dtype)
    head_of_lane = [(lane >= h * HEAD_DIM) & (lane < (h + 1) * HEAD_DIM) for h in range(NA_HEADS)]

    k0, s = [], []
    for rr in range(rows_per_step):
        r = jb * rows_per_step + rr
        rstart = jnp.clip(r - kh // 2, 0, rows_n - kh)
        k0.append(pl.multiple_of(rstart * GRID_W, GRID_W))
        q = q_ref[rr * GRID_W:(rr + 1) * GRID_W, :] * (HEAD_DIM ** -0.5)
        q4 = jnp.concatenate([jnp.where(m, q, zero) for m in head_of_lane], axis=0)
        s.append(_dot_nt(q4, k_ref[pl.ds(k0[rr], slab), :]) + bias_ref[r - rstart])
    p, l = [], []
    for rr in range(rows_per_step):
        e = jnp.exp(s[rr] - jnp.max(s[rr], axis=1, keepdims=True))
        l.append(jnp.sum(e, axis=1, keepdims=True))
        p.append(e.astype(BF16))
    for rr in range(rows_per_step):
        o4 = _dot(p[rr], v_ref[pl.ds(k0[rr], slab), :]) / l[rr]
        o = o4[(NA_HEADS - 1) * GRID_W:]
        for h in reversed(range(NA_HEADS - 1)):
            o = jnp.where(lane < (h + 1) * HEAD_DIM, o4[h * GRID_W:(h + 1) * GRID_W], o)
        y_ref[rr * GRID_W:(rr + 1) * GRID_W, :] = o.astype(y_ref.dtype)


def _na_mixer(na3, bias, rows_per_step):
    nb, seq, _ = na3.shape
    rows_n = seq // GRID_W
    tq = rows_per_step * GRID_W
    kern = functools.partial(_na_kernel, rows_per_step=rows_per_step, rows_n=rows_n)
    return pl.pallas_call(
        kern,
        grid=(nb, rows_n // rows_per_step),
        in_specs=[pl.BlockSpec((None, tq, NA_WIDTH), lambda b, j: (b, j, 0)),
                  pl.BlockSpec((None, seq, NA_WIDTH), lambda b, j: (b, 0, 1)),
                  pl.BlockSpec((None, seq, NA_WIDTH), lambda b, j: (b, 0, 2)),
                  pl.BlockSpec(bias.shape, lambda b, j: (0, 0, 0))],
        out_specs=pl.BlockSpec((None, tq, NA_WIDTH), lambda b, j: (b, j, 0)),
        out_shape=jax.ShapeDtypeStruct((nb, seq, NA_WIDTH), BF16),
        compiler_params=_params("parallel", "arbitrary"),
        name="na",
    )(na3, na3, na3, bias)


def _layer_norm(z, g, b):
    mu = jnp.mean(z, axis=-1, keepdims=True)
    zc = z - mu
    var = jnp.mean(zc * zc, axis=-1, keepdims=True)
    return zc * lax.rsqrt(var + LN_EPS) * g + b


FFN_CHUNK = 256


def _mix_ffn_kernel(x_ref, yl_ref, yr_ref, yn_ref, wo_ref, g1_ref, b1_ref, wg_ref, wu_ref, wd_ref,
                    g2_ref, b2_ref, o_ref, acc_ref, *, alpha):
    y = jnp.concatenate([yl_ref[...], yr_ref[...], yn_ref[...]], axis=1)
    x1 = _layer_norm(alpha * x_ref[...] + _dot(y, wo_ref[...]), g1_ref[...], b1_ref[...])
    xb = x1.astype(BF16)
    acc_ref[...] = alpha * x1
    nchunk = wg_ref.shape[1] // FFN_CHUNK
    for c in range(nchunk):
        cols = slice(c * FFN_CHUNK, (c + 1) * FFN_CHUNK)
        gate = _dot(xb, wg_ref[:, cols])
        up = _dot(xb, wu_ref[:, cols])
        hid = (gate * _sigmoid(gate) * up).astype(BF16)
        acc_ref[...] += _dot(hid, wd_ref[cols, :])
    o_ref[...] = _layer_norm(acc_ref[...], g2_ref[...], b2_ref[...])


def _mix_ffn(x2, yl, yr, yn, wo_bf, g1, b1, wg_bf, wu_bf, wd_bf, g2, b2, alpha, tm):
    n, d = x2.shape
    row = lambda width: pl.BlockSpec((tm, width), lambda i: (i, 0))
    const = lambda a: pl.BlockSpec(a.shape, lambda i: (0, 0), pipeline_mode=pl.Buffered(1))
    return pl.pallas_call(
        functools.partial(_mix_ffn_kernel, alpha=alpha),
        grid=(n // tm,),
        in_specs=[row(d), row(yl.shape[1]), row(yr.shape[1]), row(yn.shape[1]),
                  const(wo_bf), const(g1), const(b1), const(wg_bf), const(wu_bf), const(wd_bf),
                  const(g2), const(b2)],
        out_specs=row(d),
        out_shape=jax.ShapeDtypeStruct((n, d), F32),
        scratch_shapes=[pltpu.VMEM((tm, d), F32)],
        compiler_params=_params("parallel"),
        name="mix_ffn_ln",
    )(x2, yl, yr, yn, wo_bf, g1, b1, wg_bf, wu_bf, wd_bf, g2, b2)


ROW_TILE = 512
LRU_TIME_BLOCK = 256
RET_TIME_BLOCK = 512
NA_ROWS_PER_STEP = 8


def kernel(x, w_in, conv_w, conv_b, lru_w_a, lru_b_a, lru_w_x, lru_b_x, lru_lam, ret_gn_w, na_rpb,
           w_out, ln1_g, ln1_b, w_gate, w_up, w_down, ln2_g, ln2_b):
    nb, seq, d = x.shape
    depth = w_in.shape[0]
    alpha = (2 * depth) ** 0.25
    n = nb * seq
    tm = min(ROW_TILE, n)
    tabs = _ret_tables(seq)
    x2 = x.reshape(n, d).astype(F32)
    for l in range(depth):
        lx2, lg2, rqkv2, rg2, na2 = _in_proj(x2, w_in[l].astype(BF16), tabs['cos'], tabs['sin'], tm)
        wg, ba, bx, nsp = _lru_weights(lru_w_a[l], lru_b_a[l], lru_w_x[l], lru_b_x[l], lru_lam[l])
        y_lru = _lru_mixer(lx2.reshape(nb, seq, -1), lg2.reshape(nb, seq, -1), conv_w[l].astype(F32),
                           conv_b[l].astype(F32)[None, :], wg, ba, bx, nsp, min(LRU_TIME_BLOCK, seq))
        y_ret = _ret_mixer(rqkv2.reshape(nb, seq, -1), rg2.reshape(nb, seq, -1), tabs,
                           ret_gn_w[l].astype(F32)[None, :], min(RET_TIME_BLOCK, seq))
        bias = _na_bias_table(na_rpb[l], seq // GRID_W)
        y_na = _na_mixer(na2.reshape(nb, seq, -1), bias, NA_ROWS_PER_STEP)
        vec = lambda a: a.astype(F32)[None, :]
        x2 = _mix_ffn(x2, y_lru.reshape(n, -1), y_ret.reshape(n, -1), y_na.reshape(n, -1),
                      w_out[l].astype(BF16), vec(ln1_g[l]), vec(ln1_b[l]),
                      w_gate[l].astype(BF16), w_up[l].astype(BF16), w_down[l].astype(BF16),
                      vec(ln2_g[l]), vec(ln2_b[l]), alpha, tm)
    return x2.reshape(nb, seq, d).astype(x.dtype)
```

```python
import functools
import math

import numpy as np
import jax
import jax.numpy as jnp
from jax import lax
from jax.experimental import pallas as pl
from jax.experimental.pallas import tpu as pltpu

F32 = jnp.float32
BF16 = jnp.bfloat16

LANES = 128
SUBLANES = 8
VMEM_LIMIT_BYTES = 56 * 1024 * 1024

GRID_W = 64
HEAD_DIM = 64
LRU_WIDTH = 384
RET_HEADS = 6
RET_WIDTH = 384
NA_HEADS = 4
NA_WIDTH = 256
CONV_WIDTH = 4
LRU_C = 8.0
ROPE_BASE = 10000.0
GN_EPS = 1e-6
NA_MAX_KH = 8
NA_KW = 16
LN_EPS = 1e-5
CHUNK = 128
HALO = SUBLANES

LRU_GROUPS = LRU_WIDTH // LANES
RET_GROUPS = RET_WIDTH // LANES


def _dot(a, b):
    return jnp.dot(a, b, preferred_element_type=F32)


def _dot_nt(a, b):
    return lax.dot_general(a, b, (((1,), (1,)), ((), ())), preferred_element_type=F32)


def _dot_tn(a, b):
    return lax.dot_general(a, b, (((0,), (0,)), ((), ())), preferred_element_type=F32)


def _params(*sem):
    return pltpu.CompilerParams(dimension_semantics=sem, vmem_limit_bytes=VMEM_LIMIT_BYTES)


def _sigmoid(z):
    return 1.0 / (1.0 + jnp.exp(-z))


def _in_proj_kernel(x_ref, w_ref, cos_ref, sin_ref, lx_ref, lg_ref, rqkv_ref, rg_ref, na_ref):
    xb = x_ref[...].astype(BF16)
    n_lru = 2 * LRU_WIDTH
    w = RET_WIDTH
    lru = _dot(xb, w_ref[:, :n_lru])
    lx_ref[...] = lru[:, :LRU_WIDTH]
    lg_ref[...] = jax.nn.gelu(lru[:, LRU_WIDTH:]).astype(lg_ref.dtype)
    ret = _dot(xb, w_ref[:, n_lru:n_lru + 4 * w])
    cos = cos_ref[...]
    sin = sin_ref[...]
    for g in range(2 * RET_GROUPS):
        lanes = slice(g * LANES, (g + 1) * LANES)
        t = _rope(ret[:, lanes], cos, sin)
        if g >= RET_GROUPS:
            t = t * (HEAD_DIM ** -0.5)
        rqkv_ref[:, lanes] = t.astype(rqkv_ref.dtype)
    rqkv_ref[:, 2 * w:] = ret[:, 2 * w:3 * w].astype(rqkv_ref.dtype)
    gate = ret[:, 3 * w:]
    rg_ref[...] = (gate * _sigmoid(gate)).astype(rg_ref.dtype)
    na_ref[...] = _dot(xb, w_ref[:, n_lru + 4 * w:]).astype(na_ref.dtype)


def _in_proj(x2, w_bf, layer, cos, sin, tm):
    n, d = x2.shape
    seq_blocks = cos.shape[0] // tm
    n_na = 3 * NA_WIDTH
    row = lambda width: pl.BlockSpec((tm, width), lambda i: (i, 0))
    pos = pl.BlockSpec((tm, LANES), lambda i: (i % seq_blocks, 0))
    return pl.pallas_call(
        _in_proj_kernel,
        grid=(n // tm,),
        in_specs=[row(d), pl.BlockSpec((None,) + w_bf.shape[1:], lambda i: (layer, 0, 0),
                                       pipeline_mode=pl.Buffered(1)),
                  pos, pos],
        out_specs=[row(LRU_WIDTH), row(LRU_WIDTH), row(3 * RET_WIDTH), row(RET_WIDTH), row(n_na)],
        out_shape=[jax.ShapeDtypeStruct((n, LRU_WIDTH), F32),
                   jax.ShapeDtypeStruct((n, LRU_WIDTH), BF16),
                   jax.ShapeDtypeStruct((n, 3 * RET_WIDTH), BF16),
                   jax.ShapeDtypeStruct((n, RET_WIDTH), BF16),
                   jax.ShapeDtypeStruct((n, n_na), BF16)],
        compiler_params=_params("parallel"),
        name="in_proj",
    )(x2, w_bf, cos, sin)


LRU_ROW_TILE = 256


def _lru_gates(xc, wg_ref, ba_ref, bx_ref, nsp_ref, g):
    z = _dot(xc.astype(BF16), wg_ref[g])
    r = 1.0 / (1.0 + jnp.exp2(z[:, :LANES] + ba_ref[:, g * LANES:(g + 1) * LANES]))
    i = 1.0 / (1.0 + jnp.exp2(z[:, LANES:] + bx_ref[:, g * LANES:(g + 1) * LANES]))
    a = jnp.exp2(r * nsp_ref[:, g * LANES:(g + 1) * LANES])
    y = 1.0 - a * a
    root = jnp.where(y > 0.0, y * lax.rsqrt(y), 0.0)
    return a, root * (i * xc)


def _lru_fwd_kernel(x_ref, prev_ref, next_ref, cw_ref, cb_ref, wg_ref, ba_ref, bx_ref, nsp_ref,
                    xc_ref, hf_ref, xs_scr, a_scr, u_scr, h_scr):
    i = pl.program_id(0)
    nblk = pl.num_programs(0)
    nb, tc, _ = x_ref.shape
    rows = tc * nb
    halo_rows = HALO * nb
    ntile = rows // LRU_ROW_TILE
    steps_per_tile = LRU_ROW_TILE // nb

    @pl.when(i == 0)
    def _():
        h_scr[...] = jnp.zeros_like(h_scr)

    has_prev = (i > 0).astype(F32)
    has_next = (i < nblk - 1).astype(F32)
    for g in range(LRU_GROUPS):
        lanes = slice(g * LANES, (g + 1) * LANES)
        for b in range(nb):
            xs_scr[g, pl.ds(b, HALO, stride=nb), :] = prev_ref[b, :, lanes] * has_prev
            xs_scr[g, pl.ds(halo_rows + b, tc, stride=nb), :] = x_ref[b, :, lanes]
            xs_scr[g, pl.ds(halo_rows + rows + b, HALO, stride=nb), :] = next_ref[b, :, lanes] * has_next

    left = CONV_WIDTH // 2

    def coeffs(k, g):
        lanes = slice(g * LANES, (g + 1) * LANES)
        r0 = pl.multiple_of(k * LRU_ROW_TILE, LRU_ROW_TILE)
        xc = jnp.broadcast_to(cb_ref[:, lanes], (LRU_ROW_TILE, LANES))
        for j in range(CONV_WIDTH):
            off = halo_rows + (j - left) * nb
            xc = xc + xs_scr[g, pl.ds(r0 + off, LRU_ROW_TILE), :] * cw_ref[j:j + 1, lanes]
        a, u = _lru_gates(xc, wg_ref, ba_ref, bx_ref, nsp_ref, g)
        xc_ref[pl.ds(r0, LRU_ROW_TILE), lanes] = xc
        a_scr[g, pl.ds(r0, LRU_ROW_TILE), :] = a
        u_scr[g, pl.ds(r0, LRU_ROW_TILE), :] = u

    def scan(k, g, h):
        lanes = slice(g * LANES, (g + 1) * LANES)
        r0 = pl.multiple_of(k * LRU_ROW_TILE, LRU_ROW_TILE)
        for t in range(steps_per_tile):
            r = r0 + t * nb
            h = a_scr[g, pl.ds(r, nb), :] * h + u_scr[g, pl.ds(r, nb), :]
            hf_ref[pl.ds(r, nb), lanes] = h
        return h

    for p in range(LRU_GROUPS + 1):
        def body(k, h, p=p):
            if p < LRU_GROUPS:
                coeffs(k, p)
            if p >= 1:
                h = scan(k, p - 1, h)
            return h

        h0 = h_scr[p - 1] if p >= 1 else jnp.zeros((nb, LANES), F32)
        h = lax.fori_loop(0, ntile, body, h0, unroll=4)
        if p >= 1:
            h_scr[p - 1] = h


def _lru_bwd_kernel(xc_ref, hf_ref, gate_ref, wg_ref, ba_ref, bx_ref, nsp_ref,
                    y_ref, a_scr, hb_scr, h_scr):
    i = pl.program_id(0)
    nb, tc, _ = gate_ref.shape
    rows = tc * nb
    ntile = rows // LRU_ROW_TILE
    steps_per_tile = LRU_ROW_TILE // nb

    @pl.when(i == 0)
    def _():
        h_scr[...] = jnp.zeros_like(h_scr)

    def coeffs(k, g):
        lanes = slice(g * LANES, (g + 1) * LANES)
        r0 = pl.multiple_of(k * LRU_ROW_TILE, LRU_ROW_TILE)
        a, u = _lru_gates(xc_ref[pl.ds(r0, LRU_ROW_TILE), lanes], wg_ref, ba_ref, bx_ref, nsp_ref, g)
        a_scr[g, pl.ds(r0, LRU_ROW_TILE), :] = a
        hb_scr[g, pl.ds(r0, LRU_ROW_TILE), :] = u

    def scan(k, g, h):
        lanes = slice(g * LANES, (g + 1) * LANES)
        r0 = pl.multiple_of(k * LRU_ROW_TILE, LRU_ROW_TILE)
        for t in reversed(range(steps_per_tile)):
            r = r0 + t * nb
            h = a_scr[g, pl.ds(r, nb), :] * h + hb_scr[g, pl.ds(r, nb), :]
            hb_scr[g, pl.ds(r, nb), :] = h + hf_ref[pl.ds(r, nb), lanes]
        return h

    for p in range(LRU_GROUPS + 1):
        def body(kk, h, p=p):
            k = ntile - 1 - kk
            if p < LRU_GROUPS:
                coeffs(k, p)
            if p >= 1:
                h = scan(k, p - 1, h)
            return h

        h0 = h_scr[p - 1] if p >= 1 else jnp.zeros((nb, LANES), F32)
        h = lax.fori_loop(0, ntile, body, h0, unroll=4)
        if p >= 1:
            h_scr[p - 1] = h

    for g in range(LRU_GROUPS):
        lanes = slice(g * LANES, (g + 1) * LANES)
        for b in range(nb):
            hsum = hb_scr[g, pl.ds(b, tc, stride=nb), :]
            y_ref[b, :, lanes] = (hsum * gate_ref[b, :, lanes].astype(F32)).astype(y_ref.dtype)


def _lru_mixer(lx3, lg3, cw, cb, wg, ba, bx, nsp, tc):
    nb, seq, _ = lx3.shape
    assert nb == SUBLANES, "the RG-LRU kernels put the batch on the sublane axis"
    nblk = seq // tc
    rows = tc * nb
    hb = tc // HALO
    const2 = lambda i: (0, 0)
    const3 = lambda i: (0, 0, 0)
    xc_tm, hf_tm = pl.pallas_call(
        _lru_fwd_kernel,
        grid=(nblk,),
        in_specs=[pl.BlockSpec((nb, tc, LRU_WIDTH), lambda i: (0, i, 0)),
                  pl.BlockSpec((nb, HALO, LRU_WIDTH), lambda i: (0, jnp.maximum(i * hb - 1, 0), 0)),
                  pl.BlockSpec((nb, HALO, LRU_WIDTH),
                               lambda i: (0, jnp.minimum((i + 1) * hb, seq // HALO - 1), 0)),
                  pl.BlockSpec(cw.shape, const2), pl.BlockSpec(cb.shape, const2),
                  pl.BlockSpec(wg.shape[1:], const3), pl.BlockSpec(ba.shape[1:], const2),
                  pl.BlockSpec(bx.shape[1:], const2), pl.BlockSpec(nsp.shape[1:], const2)],
        out_specs=[pl.BlockSpec((rows, LRU_WIDTH), lambda i: (i, 0)),
                   pl.BlockSpec((rows, LRU_WIDTH), lambda i: (i, 0))],
        out_shape=[jax.ShapeDtypeStruct((seq * nb, LRU_WIDTH), F32),
                   jax.ShapeDtypeStruct((seq * nb, LRU_WIDTH), F32)],
        scratch_shapes=[pltpu.VMEM((LRU_GROUPS, (tc + 2 * HALO) * nb, LANES), F32),
                        pltpu.VMEM((LRU_GROUPS, rows, LANES), F32),
                        pltpu.VMEM((LRU_GROUPS, rows, LANES), F32),
                        pltpu.VMEM((LRU_GROUPS, nb, LANES), F32)],
        compiler_params=_params("arbitrary"),
        name="lru_fwd",
    )(lx3, lx3, lx3, cw, cb, wg[0], ba[0], bx[0], nsp[0])

    rev = lambda i: (nblk - 1 - i, 0)
    return pl.pallas_call(
        _lru_bwd_kernel,
        grid=(nblk,),
        in_specs=[pl.BlockSpec((rows, LRU_WIDTH), rev),
                  pl.BlockSpec((rows, LRU_WIDTH), rev),
                  pl.BlockSpec((nb, tc, LRU_WIDTH), lambda i: (0, nblk - 1 - i, 0)),
                  pl.BlockSpec(wg.shape[1:], const3), pl.BlockSpec(ba.shape[1:], const2),
                  pl.BlockSpec(bx.shape[1:], const2), pl.BlockSpec(nsp.shape[1:], const2)],
        out_specs=pl.BlockSpec((nb, tc, LRU_WIDTH), lambda i: (0, nblk - 1 - i, 0)),
        out_shape=jax.ShapeDtypeStruct((nb, seq, LRU_WIDTH), BF16),
        scratch_shapes=[pltpu.VMEM((LRU_GROUPS, rows, LANES), F32),
                        pltpu.VMEM((LRU_GROUPS, rows, LANES), F32),
                        pltpu.VMEM((LRU_GROUPS, nb, LANES), F32)],
        compiler_params=_params("arbitrary"),
        name="lru_bwd",
    )(xc_tm, hf_tm, lg3, wg[1], ba[1], bx[1], nsp[1])


def _lru_weights(w_a, b_a, w_x, b_x, lam):
    nblocks = w_a.shape[1]
    per_group = LANES // w_a.shape[2]

    def blockdiag(w):
        w = w.reshape(2, nblocks // per_group, per_group, w.shape[2], w.shape[3])
        eye = jnp.eye(per_group, dtype=w.dtype)
        w = w[:, :, :, :, None, :] * eye[None, None, :, None, :, None]
        return w.reshape(2, nblocks // per_group, LANES, LANES)

    log2e = math.log2(math.e)
    wg = (jnp.concatenate([blockdiag(w_a), blockdiag(w_x)], axis=-1) * -log2e).astype(BF16)
    nsp = (-LRU_C * log2e) * jax.nn.softplus(-lam.astype(F32))
    return (wg, (b_a.astype(F32) * -log2e)[:, None, :], (b_x.astype(F32) * -log2e)[:, None, :],
            nsp[:, None, :])


def _ret_tables(seq):
    c = CHUNK
    log_g = jnp.log1p(-jnp.exp2(-5.0 - jnp.arange(RET_HEADS, dtype=F32)))
    lg_lane = jnp.repeat(log_g, HEAD_DIM)[None, :]
    idx = jnp.arange(c, dtype=F32)[:, None]
    t = {}
    t['q_f'] = jnp.exp((idx + 1.0) * lg_lane)
    t['q_b'] = jnp.exp((c - idx) * lg_lane)
    t['k_f'] = jnp.exp((c - 1 - idx) * lg_lane)
    t['k_b'] = jnp.exp(idx * lg_lane)
    dec = jnp.exp(jnp.abs(idx - idx.T)[None] * log_g[:, None, None])
    t['decay'] = dec.reshape(RET_GROUPS, 2, c, c).transpose(0, 2, 1, 3).reshape(RET_GROUPS, c, 2 * c)
    li = np.arange(LANES)
    same = jnp.asarray((li[:, None] // HEAD_DIM) == (li[None, :] // HEAD_DIM), F32)
    g_chunk = jnp.exp(c * log_g).reshape(RET_GROUPS, LANES // HEAD_DIM)
    t['gch'] = jnp.repeat(g_chunk, HEAD_DIM, axis=1)[:, :, None] * same[None]
    t['bd'] = same[None]
    half = HEAD_DIM // 2
    inv_freq = ROPE_BASE ** (-jnp.arange(half, dtype=F32) / half)
    ang = jnp.arange(seq, dtype=F32)[:, None] * inv_freq[None, :]
    cos, sin = jnp.cos(ang), jnp.sin(ang)
    reps = LANES // HEAD_DIM
    t['cos'] = jnp.tile(jnp.concatenate([cos, cos], axis=1), (1, reps))
    t['sin'] = jnp.tile(jnp.concatenate([-sin, sin], axis=1), (1, reps))
    return t


def _rope(t, cos, sin):
    half = HEAD_DIM // 2
    lane = lax.broadcasted_iota(jnp.int32, t.shape, 1)
    first = (lane % HEAD_DIM) < half
    partner = jnp.where(first, pltpu.roll(t, LANES - half, 1), pltpu.roll(t, half, 1))
    return t * cos + partner * sin


def _split_heads(t, lo):
    zero = jnp.zeros_like(t)
    return jnp.concatenate([jnp.where(lo, t, zero), jnp.where(lo, zero, t)], axis=0)


def _ret_blk(ref, c, g):
    return ref[c * CHUNK:(c + 1) * CHUNK, g * LANES:(g + 1) * LANES]


def _ret_bwd_kernel(k_ref, v_ref, kb_ref, gch_ref, bd_ref, st_ref, s_scr):
    j = pl.program_id(1)

    @pl.when(j == 0)
    def _():
        s_scr[...] = jnp.zeros_like(s_scr)

    nchunk = k_ref.shape[0] // CHUNK
    kv = {}
    for c in range(nchunk):
        for g in range(RET_GROUPS):
            kb = kb_ref[:, g * LANES:(g + 1) * LANES]
            kv[c, g] = _dot_tn((_ret_blk(k_ref, c, g).astype(F32) * kb).astype(BF16), _ret_blk(v_ref, c, g))
    for g in range(RET_GROUPS):
        st = s_scr[g]
        for c in reversed(range(nchunk)):
            st_ref[c, g] = st.astype(st_ref.dtype)
            st = gch_ref[g] * st + bd_ref[0] * kv[c, g]
        s_scr[g] = st


def _ret_fwd_kernel(q_ref, k_ref, v_ref, g_ref, qf_ref, qb_ref, kf_ref, dec_ref,
                    gch_ref, bd_ref, gnw_ref, stb_ref, y_ref, s_scr):
    j = pl.program_id(1)

    @pl.when(j == 0)
    def _():
        s_scr[...] = jnp.zeros_like(s_scr)

    nchunk = q_ref.shape[0] // CHUNK
    units = [(c, g) for c in range(nchunk) for g in range(RET_GROUPS)]
    lo = lax.broadcasted_iota(jnp.int32, (CHUNK, LANES), 1) < HEAD_DIM
    gn_avg = bd_ref[0].astype(BF16)

    s, kv = {}, {}
    for c, g in units:
        lanes = slice(g * LANES, (g + 1) * LANES)
        k = _ret_blk(k_ref, c, g)
        s[c, g] = _dot_nt(_ret_blk(q_ref, c, g), _split_heads(k.astype(F32), lo).astype(BF16))
        kv[c, g] = _dot_tn((k.astype(F32) * kf_ref[:, lanes]).astype(BF16), _ret_blk(v_ref, c, g))
    o = {}
    for c, g in units:
        v2 = _split_heads(_ret_blk(v_ref, c, g).astype(F32), lo).astype(BF16)
        o[c, g] = _dot((s[c, g] * dec_ref[g]).astype(BF16), v2)
    for g in range(RET_GROUPS):
        lanes = slice(g * LANES, (g + 1) * LANES)
        st = s_scr[g]
        for c in range(nchunk):
            q = _ret_blk(q_ref, c, g).astype(F32)
            q2 = jnp.concatenate([(q * qf_ref[:, lanes]).astype(BF16), (q * qb_ref[:, lanes]).astype(BF16)],
                                 axis=1)
            st2 = jnp.concatenate([st.astype(BF16), stb_ref[c, g]], axis=0)
            o[c, g] = o[c, g] + _dot(q2, st2)
            st = gch_ref[g] * st + bd_ref[0] * kv[c, g]
        s_scr[g] = st
    for g in range(RET_GROUPS):
        lanes = slice(g * LANES, (g + 1) * LANES)
        og = jnp.concatenate([o[c, g] for c in range(nchunk)], axis=0)
        oc = og - _dot(og.astype(BF16), gn_avg) * (1.0 / HEAD_DIM)
        var = _dot((oc * oc).astype(BF16), gn_avg) * (1.0 / HEAD_DIM)
        y = g_ref[:, lanes].astype(F32) * (oc * lax.rsqrt(var + GN_EPS) * gnw_ref[:, lanes])
        y_ref[:, lanes] = y.astype(y_ref.dtype)


def _ret_mixer(rqkv3, rg3, tabs, gn_w, tb, tb_bwd):
    nb, seq, _ = rqkv3.shape
    nblk = seq // tb
    cpb = tb // CHUNK
    w = RET_WIDTH
    const2 = lambda b, j: (0, 0)
    const3 = lambda b, j: (0, 0, 0)
    tab2 = lambda name: pl.BlockSpec(tabs[name].shape, const2)
    tab3 = lambda name: pl.BlockSpec(tabs[name].shape, const3)

    col = lambda cidx: pl.BlockSpec((None, tb, w), lambda b, j: (b, j, cidx))

    nblk_bwd = seq // tb_bwd
    rev = lambda cidx: pl.BlockSpec((None, tb_bwd, w), lambda b, j: (b, nblk_bwd - 1 - j, cidx))
    st_b = pl.pallas_call(
        _ret_bwd_kernel,
        grid=(nb, nblk_bwd),
        in_specs=[rev(1), rev(2), tab2('k_b'), tab3('gch'), tab3('bd')],
        out_specs=pl.BlockSpec((None, tb_bwd // CHUNK, RET_GROUPS, LANES, LANES),
                               lambda b, j: (b, nblk_bwd - 1 - j, 0, 0, 0)),
        out_shape=jax.ShapeDtypeStruct((nb, seq // CHUNK, RET_GROUPS, LANES, LANES), BF16),
        scratch_shapes=[pltpu.VMEM((RET_GROUPS, LANES, LANES), F32)],
        compiler_params=_params("parallel", "arbitrary"),
        name="ret_bwd",
    )(rqkv3, rqkv3, tabs['k_b'], tabs['gch'], tabs['bd'])

    return pl.pallas_call(
        _ret_fwd_kernel,
        grid=(nb, nblk),
        in_specs=[col(0), col(1), col(2), col(0),
                  tab2('q_f'), tab2('q_b'), tab2('k_f'), tab3('decay'), tab3('gch'), tab3('bd'),
                  pl.BlockSpec(gn_w.shape, const2),
                  pl.BlockSpec((None, cpb, RET_GROUPS, LANES, LANES), lambda b, j: (b, j, 0, 0, 0))],
        out_specs=pl.BlockSpec((None, tb, w), lambda b, j: (b, j, 0)),
        out_shape=jax.ShapeDtypeStruct((nb, seq, w), BF16),
        scratch_shapes=[pltpu.VMEM((RET_GROUPS, LANES, LANES), F32)],
        compiler_params=_params("parallel", "arbitrary"),
        name="ret_fwd",
    )(rqkv3, rqkv3, rqkv3, rg3, tabs['q_f'], tabs['q_b'], tabs['k_f'],
      tabs['decay'], tabs['gch'], tabs['bd'], gn_w, st_b)


def _na_bias_table(rpb, rows_n):
    kh = min(NA_MAX_KH, rows_n)
    var = np.arange(kh)
    j = np.arange(kh)
    dr = j[None, :] - var[:, None]
    ri = dr + NA_MAX_KH - 1
    c = np.arange(GRID_W)
    cstart = np.clip(c - NA_KW // 2, 0, GRID_W - NA_KW)
    kc = np.arange(GRID_W)
    dc = kc[None, :] - c[:, None]
    ci = np.clip(dc, 1 - NA_KW, NA_KW - 1) + NA_KW - 1
    valid = (kc[None, :] >= cstart[:, None]) & (kc[None, :] < cstart[:, None] + NA_KW)
    row_sel = np.zeros((kh, kh, rpb.shape[2]), np.float32)
    row_sel[var[:, None], j[None, :], ri] = 1.0
    col_sel = np.zeros((GRID_W, GRID_W, rpb.shape[3]), np.float32)
    col_sel[c[:, None], kc[None, :], ci] = 1.0
    b = jnp.einsum('vja,lhab,qkb->lvhqjk', row_sel, rpb.astype(F32), col_sel,
                   precision=lax.Precision.HIGHEST)
    b = jnp.where(valid[None, None, None, :, None, :], b, -jnp.inf)
    return b.reshape(rpb.shape[0], kh, NA_HEADS * GRID_W, kh * GRID_W)


def _na_kernel(q_ref, k_ref, v_ref, bias_ref, y_ref, *, rows_per_step, rows_n):
    jb = pl.program_id(1)
    kh = min(NA_MAX_KH, rows_n)
    slab = kh * GRID_W
    lane = lax.broadcasted_iota(jnp.int32, (GRID_W, NA_WIDTH), 1)
    zero = jnp.zeros((GRID_W, NA_WIDTH), q_ref.dtype)
    head_of_lane = [(lane >= h * HEAD_DIM) & (lane < (h + 1) * HEAD_DIM) for h in range(NA_HEADS)]

    k0, s = [], []
    for rr in range(rows_per_step):
        r = jb * rows_per_step + rr
        rstart = jnp.clip(r - kh // 2, 0, rows_n - kh)
        k0.append(pl.multiple_of(rstart * GRID_W, GRID_W))
        q = q_ref[rr * GRID_W:(rr + 1) * GRID_W, :] * (HEAD_DIM ** -0.5)
        q4 = jnp.concatenate([jnp.where(m, q, zero) for m in head_of_lane], axis=0)
        s.append(_dot_nt(q4, k_ref[pl.ds(k0[rr], slab), :]) + bias_ref[r - rstart])
    p, l = [], []
    for rr in range(rows_per_step):
        e = jnp.exp(s[rr] - jnp.max(s[rr], axis=1, keepdims=True))
        l.append(jnp.sum(e, axis=1, keepdims=True))
        p.append(e.astype(BF16))
    for rr in range(rows_per_step):
        o4 = _dot(p[rr], v_ref[pl.ds(k0[rr], slab), :]) / l[rr]
        o = o4[(NA_HEADS - 1) * GRID_W:]
        for h in reversed(range(NA_HEADS - 1)):
            o = jnp.where(lane < (h + 1) * HEAD_DIM, o4[h * GRID_W:(h + 1) * GRID_W], o)
        y_ref[rr * GRID_W:(rr + 1) * GRID_W, :] = o.astype(y_ref.dtype)


def _na_mixer(na3, bias, layer, rows_per_step):
    nb, seq, _ = na3.shape
    rows_n = seq // GRID_W
    tq = rows_per_step * GRID_W
    kern = functools.partial(_na_kernel, rows_per_step=rows_per_step, rows_n=rows_n)
    return pl.pallas_call(
        kern,
        grid=(nb, rows_n // rows_per_step),
        in_specs=[pl.BlockSpec((None, tq, NA_WIDTH), lambda b, j: (b, j, 0)),
                  pl.BlockSpec((None, seq, NA_WIDTH), lambda b, j: (b, 0, 1)),
                  pl.BlockSpec((None, seq, NA_WIDTH), lambda b, j: (b, 0, 2)),
                  pl.BlockSpec((None,) + bias.shape[1:], lambda b, j: (layer, 0, 0, 0))],
        out_specs=pl.BlockSpec((None, tq, NA_WIDTH), lambda b, j: (b, j, 0)),
        out_shape=jax.ShapeDtypeStruct((nb, seq, NA_WIDTH), BF16),
        compiler_params=_params("parallel", "arbitrary"),
        name="na",
    )(na3, na3, na3, bias)


def _layer_norm(z, g, b):
    mu = jnp.mean(z, axis=-1, keepdims=True)
    zc = z - mu
    var = jnp.mean(zc * zc, axis=-1, keepdims=True)
    return zc * lax.rsqrt(var + LN_EPS) * g + b


FFN_CHUNK = 256


def _mix_ffn_kernel(x_ref, yl_ref, yr_ref, yn_ref, wo_ref, g1_ref, b1_ref, wg_ref, wu_ref, wd_ref,
                    g2_ref, b2_ref, o_ref, acc_ref, *, alpha):
    y = jnp.concatenate([yl_ref[...], yr_ref[...], yn_ref[...]], axis=1)
    x1 = _layer_norm(alpha * x_ref[...] + _dot(y, wo_ref[...]), g1_ref[...], b1_ref[...])
    xb = x1.astype(BF16)
    acc_ref[...] = alpha * x1
    nchunk = wg_ref.shape[1] // FFN_CHUNK
    for c in range(nchunk):
        cols = slice(c * FFN_CHUNK, (c + 1) * FFN_CHUNK)
        gate = _dot(xb, wg_ref[:, cols])
        up = _dot(xb, wu_ref[:, cols])
        hid = (gate * _sigmoid(gate) * up).astype(BF16)
        acc_ref[...] += _dot(hid, wd_ref[cols, :])
    o_ref[...] = _layer_norm(acc_ref[...], g2_ref[...], b2_ref[...])


def _mix_ffn(x2, yl, yr, yn, wo_bf, g1, b1, wg_bf, wu_bf, wd_bf, g2, b2, layer, alpha, tm):
    n, d = x2.shape
    row = lambda width: pl.BlockSpec((tm, width), lambda i: (i, 0))
    const = lambda a: pl.BlockSpec((None,) + a.shape[1:], lambda i: (layer, 0, 0),
                                   pipeline_mode=pl.Buffered(1))
    return pl.pallas_call(
        functools.partial(_mix_ffn_kernel, alpha=alpha),
        grid=(n // tm,),
        in_specs=[row(d), row(yl.shape[1]), row(yr.shape[1]), row(yn.shape[1]),
                  const(wo_bf), const(g1), const(b1), const(wg_bf), const(wu_bf), const(wd_bf),
                  const(g2), const(b2)],
        out_specs=row(d),
        out_shape=jax.ShapeDtypeStruct((n, d), F32),
        scratch_shapes=[pltpu.VMEM((tm, d), F32)],
        compiler_params=_params("parallel"),
        name="mix_ffn_ln",
    )(x2, yl, yr, yn, wo_bf, g1, b1, wg_bf, wu_bf, wd_bf, g2, b2)


ROW_TILE = 512
LRU_TIME_BLOCK = 256
RET_TIME_BLOCK = 512
RET_BWD_TIME_BLOCK = 2048
NA_ROWS_PER_STEP = 8


def kernel(x, w_in, conv_w, conv_b, lru_w_a, lru_b_a, lru_w_x, lru_b_x, lru_lam, ret_gn_w, na_rpb,
           w_out, ln1_g, ln1_b, w_gate, w_up, w_down, ln2_g, ln2_b):
    nb, seq, d = x.shape
    depth = w_in.shape[0]
    alpha = (2 * depth) ** 0.25
    n = nb * seq
    tm = min(ROW_TILE, n)
    tabs = _ret_tables(seq)
    bias = _na_bias_table(na_rpb, seq // GRID_W)
    w_in_bf, w_out_bf = w_in.astype(BF16), w_out.astype(BF16)
    w_gate_bf, w_up_bf, w_down_bf = w_gate.astype(BF16), w_up.astype(BF16), w_down.astype(BF16)
    vec = lambda a: a.astype(F32)[:, None, :]
    ln = [vec(a) for a in (ln1_g, ln1_b, ln2_g, ln2_b)]
    x2 = x.reshape(n, d).astype(F32)
    for l in range(depth):
        lx2, lg2, rqkv2, rg2, na2 = _in_proj(x2, w_in_bf, l, tabs['cos'], tabs['sin'], tm)
        wg, ba, bx, nsp = _lru_weights(lru_w_a[l], lru_b_a[l], lru_w_x[l], lru_b_x[l], lru_lam[l])
        y_lru = _lru_mixer(lx2.reshape(nb, seq, -1), lg2.reshape(nb, seq, -1), conv_w[l].astype(F32),
                           conv_b[l].astype(F32)[None, :], wg, ba, bx, nsp, min(LRU_TIME_BLOCK, seq))
        y_ret = _ret_mixer(rqkv2.reshape(nb, seq, -1), rg2.reshape(nb, seq, -1), tabs,
                           ret_gn_w[l].astype(F32)[None, :], min(RET_TIME_BLOCK, seq),
                           min(RET_BWD_TIME_BLOCK, seq))
        y_na = _na_mixer(na2.reshape(nb, seq, -1), bias, l, NA_ROWS_PER_STEP)
        x2 = _mix_ffn(x2, y_lru.reshape(n, -1), y_ret.reshape(n, -1), y_na.reshape(n, -1),
                      w_out_bf, ln[0], ln[1], w_gate_bf, w_up_bf, w_down_bf, ln[2], ln[3], l, alpha, tm)
    return x2.reshape(nb, seq, d).astype(x.dtype)
```

```python
import functools
import math

import numpy as np
import jax
import jax.numpy as jnp
from jax import lax
from jax.experimental import pallas as pl
from jax.experimental.pallas import tpu as pltpu

F32 = jnp.float32
BF16 = jnp.bfloat16

LANES = 128
SUBLANES = 8
VMEM_LIMIT_BYTES = 56 * 1024 * 1024

GRID_W = 64
HEAD_DIM = 64
LRU_WIDTH = 384
RET_HEADS = 6
RET_WIDTH = 384
NA_HEADS = 4
NA_WIDTH = 256
CONV_WIDTH = 4
LRU_C = 8.0
ROPE_BASE = 10000.0
GN_EPS = 1e-6
NA_MAX_KH = 8
NA_KW = 16
LN_EPS = 1e-5
CHUNK = 128
HALO = SUBLANES

LRU_GROUPS = LRU_WIDTH // LANES
RET_GROUPS = RET_WIDTH // LANES


def _dot(a, b):
    return jnp.dot(a, b, preferred_element_type=F32)


def _dot_nt(a, b):
    return lax.dot_general(a, b, (((1,), (1,)), ((), ())), preferred_element_type=F32)


def _dot_tn(a, b):
    return lax.dot_general(a, b, (((0,), (0,)), ((), ())), preferred_element_type=F32)


def _params(*sem):
    return pltpu.CompilerParams(dimension_semantics=sem, vmem_limit_bytes=VMEM_LIMIT_BYTES)


def _sigmoid(z):
    return 1.0 / (1.0 + jnp.exp(-z))


def _in_proj_kernel(x_ref, w_ref, cos_ref, sin_ref, lx_ref, lg_ref, rqkv_ref, rg_ref, na_ref):
    xb = x_ref[...].astype(BF16)
    n_lru = 2 * LRU_WIDTH
    w = RET_WIDTH
    lru = _dot(xb, w_ref[:, :n_lru])
    lx_ref[...] = lru[:, :LRU_WIDTH]
    lg_ref[...] = jax.nn.gelu(lru[:, LRU_WIDTH:]).astype(lg_ref.dtype)
    ret = _dot(xb, w_ref[:, n_lru:n_lru + 4 * w])
    cos = cos_ref[...]
    sin = sin_ref[...]
    for g in range(2 * RET_GROUPS):
        lanes = slice(g * LANES, (g + 1) * LANES)
        t = _rope(ret[:, lanes], cos, sin)
        if g >= RET_GROUPS:
            t = t * (HEAD_DIM ** -0.5)
        rqkv_ref[:, lanes] = t.astype(rqkv_ref.dtype)
    rqkv_ref[:, 2 * w:] = ret[:, 2 * w:3 * w].astype(rqkv_ref.dtype)
    gate = ret[:, 3 * w:]
    rg_ref[...] = (gate * _sigmoid(gate)).astype(rg_ref.dtype)
    na_ref[...] = _dot(xb, w_ref[:, n_lru + 4 * w:]).astype(na_ref.dtype)


def _in_proj(x2, w_bf, layer, cos, sin, tm):
    n, d = x2.shape
    assert cos.shape[0] % tm == 0, "a row tile must not straddle two sequences (rotary position blocks)"
    seq_blocks = cos.shape[0] // tm
    n_na = 3 * NA_WIDTH
    row = lambda width: pl.BlockSpec((tm, width), lambda i: (i, 0))
    pos = pl.BlockSpec((tm, LANES), lambda i: (i % seq_blocks, 0))
    return pl.pallas_call(
        _in_proj_kernel,
        grid=(n // tm,),
        in_specs=[row(d), pl.BlockSpec((None,) + w_bf.shape[1:], lambda i: (layer, 0, 0),
                                       pipeline_mode=pl.Buffered(1)),
                  pos, pos],
        out_specs=[row(LRU_WIDTH), row(LRU_WIDTH), row(3 * RET_WIDTH), row(RET_WIDTH), row(n_na)],
        out_shape=[jax.ShapeDtypeStruct((n, LRU_WIDTH), F32),
                   jax.ShapeDtypeStruct((n, LRU_WIDTH), BF16),
                   jax.ShapeDtypeStruct((n, 3 * RET_WIDTH), BF16),
                   jax.ShapeDtypeStruct((n, RET_WIDTH), BF16),
                   jax.ShapeDtypeStruct((n, n_na), BF16)],
        compiler_params=_params("parallel"),
        name="in_proj",
    )(x2, w_bf, cos, sin)


LRU_ROW_TILE = 256


def _lru_gates(xc, wg_ref, ba_ref, bx_ref, nsp_ref, g):
    z = _dot(xc.astype(BF16), wg_ref[g])
    r = 1.0 / (1.0 + jnp.exp2(z[:, :LANES] + ba_ref[:, g * LANES:(g + 1) * LANES]))
    i = 1.0 / (1.0 + jnp.exp2(z[:, LANES:] + bx_ref[:, g * LANES:(g + 1) * LANES]))
    a = jnp.exp2(r * nsp_ref[:, g * LANES:(g + 1) * LANES])
    y = 1.0 - a * a
    root = jnp.where(y > 0.0, y * lax.rsqrt(y), 0.0)
    return a, root * (i * xc)


def _lru_fwd_kernel(x_ref, prev_ref, next_ref, cw_ref, cb_ref, wg_ref, ba_ref, bx_ref, nsp_ref,
                    xc_ref, hf_ref, xs_scr, a_scr, u_scr, h_scr):
    i = pl.program_id(0)
    nblk = pl.num_programs(0)
    nb, tc, _ = x_ref.shape
    rows = tc * nb
    halo_rows = HALO * nb
    ntile = rows // LRU_ROW_TILE
    steps_per_tile = LRU_ROW_TILE // nb

    @pl.when(i == 0)
    def _():
        h_scr[...] = jnp.zeros_like(h_scr)

    has_prev = (i > 0).astype(F32)
    has_next = (i < nblk - 1).astype(F32)
    for g in range(LRU_GROUPS):
        lanes = slice(g * LANES, (g + 1) * LANES)
        for b in range(nb):
            xs_scr[g, pl.ds(b, HALO, stride=nb), :] = prev_ref[b, :, lanes] * has_prev
            xs_scr[g, pl.ds(halo_rows + rows + b, HALO, stride=nb), :] = next_ref[b, :, lanes] * has_next
            if g == 0:
                xs_scr[g, pl.ds(halo_rows + b, tc, stride=nb), :] = x_ref[b, :, lanes]

    def to_time_major(k, g):
        lanes = slice(g * LANES, (g + 1) * LANES)
        t0 = pl.multiple_of(k * steps_per_tile, steps_per_tile)
        r0 = pl.multiple_of(k * LRU_ROW_TILE, LRU_ROW_TILE)
        for b in range(nb):
            xs_scr[g, pl.ds(halo_rows + r0 + b, steps_per_tile, stride=nb), :] = (
                x_ref[b, pl.ds(t0, steps_per_tile), lanes])

    left = CONV_WIDTH // 2

    def coeffs(k, g):
        lanes = slice(g * LANES, (g + 1) * LANES)
        r0 = pl.multiple_of(k * LRU_ROW_TILE, LRU_ROW_TILE)
        xc = jnp.broadcast_to(cb_ref[:, lanes], (LRU_ROW_TILE, LANES))
        for j in range(CONV_WIDTH):
            off = halo_rows + (j - left) * nb
            xc = xc + xs_scr[g, pl.ds(r0 + off, LRU_ROW_TILE), :] * cw_ref[j:j + 1, lanes]
        a, u = _lru_gates(xc, wg_ref, ba_ref, bx_ref, nsp_ref, g)
        xc_ref[pl.ds(r0, LRU_ROW_TILE), lanes] = xc
        a_scr[g, pl.ds(r0, LRU_ROW_TILE), :] = a
        u_scr[g, pl.ds(r0, LRU_ROW_TILE), :] = u

    def scan(k, g, h):
        lanes = slice(g * LANES, (g + 1) * LANES)
        r0 = pl.multiple_of(k * LRU_ROW_TILE, LRU_ROW_TILE)
        for t in range(steps_per_tile):
            r = r0 + t * nb
            h = a_scr[g, pl.ds(r, nb), :] * h + u_scr[g, pl.ds(r, nb), :]
            hf_ref[pl.ds(r, nb), lanes] = h
        return h

    for p in range(LRU_GROUPS + 1):
        def body(k, h, p=p):
            if p < LRU_GROUPS:
                coeffs(k, p)
            if p >= 1:
                h = scan(k, p - 1, h)
            if p + 1 < LRU_GROUPS:
                to_time_major(k, p + 1)
            return h

        h0 = h_scr[p - 1] if p >= 1 else jnp.zeros((nb, LANES), F32)
        h = lax.fori_loop(0, ntile, body, h0, unroll=4)
        if p >= 1:
            h_scr[p - 1] = h


def _lru_bwd_kernel(xc_ref, hf_ref, gate_ref, wg_ref, ba_ref, bx_ref, nsp_ref,
                    y_ref, a_scr, hb_scr, h_scr):
    i = pl.program_id(0)
    nb, tc, _ = gate_ref.shape
    rows = tc * nb
    ntile = rows // LRU_ROW_TILE
    steps_per_tile = LRU_ROW_TILE // nb

    @pl.when(i == 0)
    def _():
        h_scr[...] = jnp.zeros_like(h_scr)

    def coeffs(k, g):
        lanes = slice(g * LANES, (g + 1) * LANES)
        r0 = pl.multiple_of(k * LRU_ROW_TILE, LRU_ROW_TILE)
        a, u = _lru_gates(xc_ref[pl.ds(r0, LRU_ROW_TILE), lanes], wg_ref, ba_ref, bx_ref, nsp_ref, g)
        a_scr[g, pl.ds(r0, LRU_ROW_TILE), :] = a
        hb_scr[g, pl.ds(r0, LRU_ROW_TILE), :] = u

    def scan(k, g, h):
        lanes = slice(g * LANES, (g + 1) * LANES)
        r0 = pl.multiple_of(k * LRU_ROW_TILE, LRU_ROW_TILE)
        for t in reversed(range(steps_per_tile)):
            r = r0 + t * nb
            h = a_scr[g, pl.ds(r, nb), :] * h + hb_scr[g, pl.ds(r, nb), :]
            hb_scr[g, pl.ds(r, nb), :] = h + hf_ref[pl.ds(r, nb), lanes]
        return h

    def to_batch_major(k, g):
        lanes = slice(g * LANES, (g + 1) * LANES)
        t0 = pl.multiple_of(k * steps_per_tile, steps_per_tile)
        r0 = pl.multiple_of(k * LRU_ROW_TILE, LRU_ROW_TILE)
        for b in range(nb):
            hsum = hb_scr[g, pl.ds(r0 + b, steps_per_tile, stride=nb), :]
            gate = gate_ref[b, pl.ds(t0, steps_per_tile), lanes].astype(F32)
            y_ref[b, pl.ds(t0, steps_per_tile), lanes] = (hsum * gate).astype(y_ref.dtype)

    for p in range(LRU_GROUPS + 2):
        def body(kk, h, p=p):
            k = ntile - 1 - kk
            if p < LRU_GROUPS:
                coeffs(k, p)
            if 1 <= p <= LRU_GROUPS:
                h = scan(k, p - 1, h)
            if p >= 2:
                to_batch_major(k, p - 2)
            return h

        scanning = 1 <= p <= LRU_GROUPS
        h0 = h_scr[p - 1] if scanning else jnp.zeros((nb, LANES), F32)
        h = lax.fori_loop(0, ntile, body, h0, unroll=4)
        if scanning:
            h_scr[p - 1] = h


def _lru_mixer(lx3, lg3, cw, cb, wg, ba, bx, nsp, tc):
    nb, seq, _ = lx3.shape
    assert nb == SUBLANES, "the RG-LRU kernels put the batch on the sublane axis"
    nblk = seq // tc
    rows = tc * nb
    hb = tc // HALO
    const2 = lambda i: (0, 0)
    const3 = lambda i: (0, 0, 0)
    xc_tm, hf_tm = pl.pallas_call(
        _lru_fwd_kernel,
        grid=(nblk,),
        in_specs=[pl.BlockSpec((nb, tc, LRU_WIDTH), lambda i: (0, i, 0)),
                  pl.BlockSpec((nb, HALO, LRU_WIDTH), lambda i: (0, jnp.maximum(i * hb - 1, 0), 0)),
                  pl.BlockSpec((nb, HALO, LRU_WIDTH),
                               lambda i: (0, jnp.minimum((i + 1) * hb, seq // HALO - 1), 0)),
                  pl.BlockSpec(cw.shape, const2), pl.BlockSpec(cb.shape, const2),
                  pl.BlockSpec(wg.shape[1:], const3), pl.BlockSpec(ba.shape[1:], const2),
                  pl.BlockSpec(bx.shape[1:], const2), pl.BlockSpec(nsp.shape[1:], const2)],
        out_specs=[pl.BlockSpec((rows, LRU_WIDTH), lambda i: (i, 0)),
                   pl.BlockSpec((rows, LRU_WIDTH), lambda i: (i, 0))],
        out_shape=[jax.ShapeDtypeStruct((seq * nb, LRU_WIDTH), F32),
                   jax.ShapeDtypeStruct((seq * nb, LRU_WIDTH), F32)],
        scratch_shapes=[pltpu.VMEM((LRU_GROUPS, (tc + 2 * HALO) * nb, LANES), F32),
                        pltpu.VMEM((LRU_GROUPS, rows, LANES), F32),
                        pltpu.VMEM((LRU_GROUPS, rows, LANES), F32),
                        pltpu.VMEM((LRU_GROUPS, nb, LANES), F32)],
        compiler_params=_params("arbitrary"),
        name="lru_fwd",
    )(lx3, lx3, lx3, cw, cb, wg[0], ba[0], bx[0], nsp[0])

    rev = lambda i: (nblk - 1 - i, 0)
    return pl.pallas_call(
        _lru_bwd_kernel,
        grid=(nblk,),
        in_specs=[pl.BlockSpec((rows, LRU_WIDTH), rev),
                  pl.BlockSpec((rows, LRU_WIDTH), rev),
                  pl.BlockSpec((nb, tc, LRU_WIDTH), lambda i: (0, nblk - 1 - i, 0)),
                  pl.BlockSpec(wg.shape[1:], const3), pl.BlockSpec(ba.shape[1:], const2),
                  pl.BlockSpec(bx.shape[1:], const2), pl.BlockSpec(nsp.shape[1:], const2)],
        out_specs=pl.BlockSpec((nb, tc, LRU_WIDTH), lambda i: (0, nblk - 1 - i, 0)),
        out_shape=jax.ShapeDtypeStruct((nb, seq, LRU_WIDTH), BF16),
        scratch_shapes=[pltpu.VMEM((LRU_GROUPS, rows, LANES), F32),
                        pltpu.VMEM((LRU_GROUPS, rows, LANES), F32),
                        pltpu.VMEM((LRU_GROUPS, nb, LANES), F32)],
        compiler_params=_params("arbitrary"),
        name="lru_bwd",
    )(xc_tm, hf_tm, lg3, wg[1], ba[1], bx[1], nsp[1])


def _lru_weights(w_a, b_a, w_x, b_x, lam):
    nblocks = w_a.shape[1]
    per_group = LANES // w_a.shape[2]

    def blockdiag(w):
        w = w.reshape(2, nblocks // per_group, per_group, w.shape[2], w.shape[3])
        eye = jnp.eye(per_group, dtype=w.dtype)
        w = w[:, :, :, :, None, :] * eye[None, None, :, None, :, None]
        return w.reshape(2, nblocks // per_group, LANES, LANES)

    log2e = math.log2(math.e)
    wg = (jnp.concatenate([blockdiag(w_a), blockdiag(w_x)], axis=-1) * -log2e).astype(BF16)
    nsp = (-LRU_C * log2e) * jax.nn.softplus(-lam.astype(F32))
    return (wg, (b_a.astype(F32) * -log2e)[:, None, :], (b_x.astype(F32) * -log2e)[:, None, :],
            nsp[:, None, :])


def _ret_tables(seq):
    c = CHUNK
    log_g = jnp.log1p(-jnp.exp2(-5.0 - jnp.arange(RET_HEADS, dtype=F32)))
    lg_lane = jnp.repeat(log_g, HEAD_DIM)[None, :]
    idx = jnp.arange(c, dtype=F32)[:, None]
    t = {}
    t['q_f'] = jnp.exp((idx + 1.0) * lg_lane)
    t['q_b'] = jnp.exp((c - idx) * lg_lane)
    t['k_f'] = jnp.exp((c - 1 - idx) * lg_lane)
    t['k_b'] = jnp.exp(idx * lg_lane)
    dec = jnp.exp(jnp.abs(idx - idx.T)[None] * log_g[:, None, None])
    t['decay'] = dec.reshape(RET_GROUPS, 2, c, c).transpose(0, 2, 1, 3).reshape(RET_GROUPS, c, 2 * c)
    li = np.arange(LANES)
    same = jnp.asarray((li[:, None] // HEAD_DIM) == (li[None, :] // HEAD_DIM), F32)
    g_chunk = jnp.exp(c * log_g).reshape(RET_GROUPS, LANES // HEAD_DIM)
    t['gch'] = jnp.repeat(g_chunk, HEAD_DIM, axis=1)[:, :, None] * same[None]
    t['bd'] = same[None]
    half = HEAD_DIM // 2
    inv_freq = ROPE_BASE ** (-jnp.arange(half, dtype=F32) / half)
    ang = jnp.arange(seq, dtype=F32)[:, None] * inv_freq[None, :]
    cos, sin = jnp.cos(ang), jnp.sin(ang)
    reps = LANES // HEAD_DIM
    t['cos'] = jnp.tile(jnp.concatenate([cos, cos], axis=1), (1, reps))
    t['sin'] = jnp.tile(jnp.concatenate([-sin, sin], axis=1), (1, reps))
    return t


def _rope(t, cos, sin):
    half = HEAD_DIM // 2
    lane = lax.broadcasted_iota(jnp.int32, t.shape, 1)
    first = (lane % HEAD_DIM) < half
    partner = jnp.where(first, pltpu.roll(t, LANES - half, 1), pltpu.roll(t, half, 1))
    return t * cos + partner * sin


def _split_heads(t, lo):
    zero = jnp.zeros_like(t)
    return jnp.concatenate([jnp.where(lo, t, zero), jnp.where(lo, zero, t)], axis=0)


def _ret_blk(ref, c, g):
    return ref[c * CHUNK:(c + 1) * CHUNK, g * LANES:(g + 1) * LANES]


def _ret_bwd_kernel(k_ref, v_ref, kb_ref, gch_ref, bd_ref, st_ref, s_scr):
    j = pl.program_id(1)

    @pl.when(j == 0)
    def _():
        s_scr[...] = jnp.zeros_like(s_scr)

    nchunk = k_ref.shape[0] // CHUNK
    kv = {}
    for c in range(nchunk):
        for g in range(RET_GROUPS):
            kb = kb_ref[:, g * LANES:(g + 1) * LANES]
            kv[c, g] = _dot_tn((_ret_blk(k_ref, c, g).astype(F32) * kb).astype(BF16), _ret_blk(v_ref, c, g))
    for g in range(RET_GROUPS):
        st = s_scr[g]
        for c in reversed(range(nchunk)):
            st_ref[c, g] = st.astype(st_ref.dtype)
            st = gch_ref[g] * st + bd_ref[0] * kv[c, g]
        s_scr[g] = st


def _ret_fwd_kernel(q_ref, k_ref, v_ref, g_ref, qf_ref, qb_ref, kf_ref, dec_ref,
                    gch_ref, bd_ref, gnw_ref, stb_ref, y_ref, s_scr):
    j = pl.program_id(1)

    @pl.when(j == 0)
    def _():
        s_scr[...] = jnp.zeros_like(s_scr)

    nchunk = q_ref.shape[0] // CHUNK
    units = [(c, g) for c in range(nchunk) for g in range(RET_GROUPS)]
    lo = lax.broadcasted_iota(jnp.int32, (CHUNK, LANES), 1) < HEAD_DIM
    gn_avg = bd_ref[0].astype(BF16)

    s, kv = {}, {}
    for c, g in units:
        lanes = slice(g * LANES, (g + 1) * LANES)
        k = _ret_blk(k_ref, c, g)
        s[c, g] = _dot_nt(_ret_blk(q_ref, c, g), _split_heads(k.astype(F32), lo).astype(BF16))
        kv[c, g] = _dot_tn((k.astype(F32) * kf_ref[:, lanes]).astype(BF16), _ret_blk(v_ref, c, g))
    o = {}
    for c, g in units:
        v2 = _split_heads(_ret_blk(v_ref, c, g).astype(F32), lo).astype(BF16)
        o[c, g] = _dot((s[c, g] * dec_ref[g]).astype(BF16), v2)
    for g in range(RET_GROUPS):
        lanes = slice(g * LANES, (g + 1) * LANES)
        st = s_scr[g]
        for c in range(nchunk):
            q = _ret_blk(q_ref, c, g).astype(F32)
            q2 = jnp.concatenate([(q * qf_ref[:, lanes]).astype(BF16), (q * qb_ref[:, lanes]).astype(BF16)],
                                 axis=1)
            st2 = jnp.concatenate([st.astype(BF16), stb_ref[c, g]], axis=0)
            o[c, g] = o[c, g] + _dot(q2, st2)
            st = gch_ref[g] * st + bd_ref[0] * kv[c, g]
        s_scr[g] = st
    for g in range(RET_GROUPS):
        lanes = slice(g * LANES, (g + 1) * LANES)
        og = jnp.concatenate([o[c, g] for c in range(nchunk)], axis=0)
        oc = og - _dot(og.astype(BF16), gn_avg) * (1.0 / HEAD_DIM)
        var = _dot((oc * oc).astype(BF16), gn_avg) * (1.0 / HEAD_DIM)
        y = g_ref[:, lanes].astype(F32) * (oc * lax.rsqrt(var + GN_EPS) * gnw_ref[:, lanes])
        y_ref[:, lanes] = y.astype(y_ref.dtype)


def _ret_mixer(rqkv3, rg3, tabs, gn_w, tb, tb_bwd):
    nb, seq, _ = rqkv3.shape
    nblk = seq // tb
    cpb = tb // CHUNK
    w = RET_WIDTH
    const2 = lambda b, j: (0, 0)
    const3 = lambda b, j: (0, 0, 0)
    tab2 = lambda name: pl.BlockSpec(tabs[name].shape, const2)
    tab3 = lambda name: pl.BlockSpec(tabs[name].shape, const3)

    col = lambda cidx: pl.BlockSpec((None, tb, w), lambda b, j: (b, j, cidx))

    nblk_bwd = seq // tb_bwd
    rev = lambda cidx: pl.BlockSpec((None, tb_bwd, w), lambda b, j: (b, nblk_bwd - 1 - j, cidx))
    st_b = pl.pallas_call(
        _ret_bwd_kernel,
        grid=(nb, nblk_bwd),
        in_specs=[rev(1), rev(2), tab2('k_b'), tab3('gch'), tab3('bd')],
        out_specs=pl.BlockSpec((None, tb_bwd // CHUNK, RET_GROUPS, LANES, LANES),
                               lambda b, j: (b, nblk_bwd - 1 - j, 0, 0, 0)),
        out_shape=jax.ShapeDtypeStruct((nb, seq // CHUNK, RET_GROUPS, LANES, LANES), BF16),
        scratch_shapes=[pltpu.VMEM((RET_GROUPS, LANES, LANES), F32)],
        compiler_params=_params("parallel", "arbitrary"),
        name="ret_bwd",
    )(rqkv3, rqkv3, tabs['k_b'], tabs['gch'], tabs['bd'])

    return pl.pallas_call(
        _ret_fwd_kernel,
        grid=(nb, nblk),
        in_specs=[col(0), col(1), col(2), col(0),
                  tab2('q_f'), tab2('q_b'), tab2('k_f'), tab3('decay'), tab3('gch'), tab3('bd'),
                  pl.BlockSpec(gn_w.shape, const2),
                  pl.BlockSpec((None, cpb, RET_GROUPS, LANES, LANES), lambda b, j: (b, j, 0, 0, 0))],
        out_specs=pl.BlockSpec((None, tb, w), lambda b, j: (b, j, 0)),
        out_shape=jax.ShapeDtypeStruct((nb, seq, w), BF16),
        scratch_shapes=[pltpu.VMEM((RET_GROUPS, LANES, LANES), F32)],
        compiler_params=_params("parallel", "arbitrary"),
        name="ret_fwd",
    )(rqkv3, rqkv3, rqkv3, rg3, tabs['q_f'], tabs['q_b'], tabs['k_f'],
      tabs['decay'], tabs['gch'], tabs['bd'], gn_w, st_b)


def _na_bias_table(rpb, rows_n):
    kh = min(NA_MAX_KH, rows_n)
    var = np.arange(kh)
    j = np.arange(kh)
    dr = j[None, :] - var[:, None]
    ri = dr + NA_MAX_KH - 1
    c = np.arange(GRID_W)
    cstart = np.clip(c - NA_KW // 2, 0, GRID_W - NA_KW)
    kc = np.arange(GRID_W)
    dc = kc[None, :] - c[:, None]
    ci = np.clip(dc, 1 - NA_KW, NA_KW - 1) + NA_KW - 1
    valid = (kc[None, :] >= cstart[:, None]) & (kc[None, :] < cstart[:, None] + NA_KW)
    col_sel = np.zeros((GRID_W, GRID_W, rpb.shape[3]), np.float32)
    col_sel[c[:, None], kc[None, :], ci] = 1.0
    t = jnp.einsum('lhab,qkb->lhaqk', rpb.astype(F32), col_sel, precision=lax.Precision.HIGHEST)
    t = jnp.where(valid[None, None, None], t, -jnp.inf)
    b = jnp.stack([jnp.concatenate([t[:, :, ri[v, jj]] for jj in range(kh)], axis=-1) for v in range(kh)],
                  axis=1)
    return b.reshape(rpb.shape[0], kh, NA_HEADS * GRID_W, kh * GRID_W)


def _na_kernel(q_ref, k_ref, v_ref, bias_ref, y_ref, *, rows_per_step, rows_n):
    jb = pl.program_id(1)
    kh = min(NA_MAX_KH, rows_n)
    slab = kh * GRID_W
    lane = lax.broadcasted_iota(jnp.int32, (GRID_W, NA_WIDTH), 1)
    zero = jnp.zeros((GRID_W, NA_WIDTH), q_ref.dtype)
    head_of_lane = [(lane >= h * HEAD_DIM) & (lane < (h + 1) * HEAD_DIM) for h in range(NA_HEADS)]

    k0, s = [], []
    for rr in range(rows_per_step):
        r = jb * rows_per_step + rr
        rstart = jnp.clip(r - kh // 2, 0, rows_n - kh)
        k0.append(pl.multiple_of(rstart * GRID_W, GRID_W))
        q = q_ref[rr * GRID_W:(rr + 1) * GRID_W, :] * (HEAD_DIM ** -0.5)
        q4 = jnp.concatenate([jnp.where(m, q, zero) for m in head_of_lane], axis=0)
        s.append(_dot_nt(q4, k_ref[pl.ds(k0[rr], slab), :]) + bias_ref[r - rstart])
    p, l = [], []
    for rr in range(rows_per_step):
        e = jnp.exp(s[rr] - jnp.max(s[rr], axis=1, keepdims=True))
        l.append(jnp.sum(e, axis=1, keepdims=True))
        p.append(e.astype(BF16))
    for rr in range(rows_per_step):
        o4 = _dot(p[rr], v_ref[pl.ds(k0[rr], slab), :]) / l[rr]
        o = o4[(NA_HEADS - 1) * GRID_W:]
        for h in reversed(range(NA_HEADS - 1)):
            o = jnp.where(lane < (h + 1) * HEAD_DIM, o4[h * GRID_W:(h + 1) * GRID_W], o)
        y_ref[rr * GRID_W:(rr + 1) * GRID_W, :] = o.astype(y_ref.dtype)


def _na_mixer(na3, bias, layer, rows_per_step):
    nb, seq, _ = na3.shape
    rows_n = seq // GRID_W
    tq = rows_per_step * GRID_W
    kern = functools.partial(_na_kernel, rows_per_step=rows_per_step, rows_n=rows_n)
    return pl.pallas_call(
        kern,
        grid=(nb, rows_n // rows_per_step),
        in_specs=[pl.BlockSpec((None, tq, NA_WIDTH), lambda b, j: (b, j, 0)),
                  pl.BlockSpec((None, seq, NA_WIDTH), lambda b, j: (b, 0, 1)),
                  pl.BlockSpec((None, seq, NA_WIDTH), lambda b, j: (b, 0, 2)),
                  pl.BlockSpec((None,) + bias.shape[1:], lambda b, j: (layer, 0, 0, 0))],
        out_specs=pl.BlockSpec((None, tq, NA_WIDTH), lambda b, j: (b, j, 0)),
        out_shape=jax.ShapeDtypeStruct((nb, seq, NA_WIDTH), BF16),
        compiler_params=_params("parallel", "arbitrary"),
        name="na",
    )(na3, na3, na3, bias)


def _layer_norm(z, g, b):
    mu = jnp.mean(z, axis=-1, keepdims=True)
    zc = z - mu
    var = jnp.mean(zc * zc, axis=-1, keepdims=True)
    return zc * lax.rsqrt(var + LN_EPS) * g + b


FFN_CHUNK = 256


def _mix_ffn_kernel(x_ref, yl_ref, yr_ref, yn_ref, wo_ref, g1_ref, b1_ref, wg_ref, wu_ref, wd_ref,
                    g2_ref, b2_ref, o_ref, acc_ref, *, alpha):
    y = jnp.concatenate([yl_ref[...], yr_ref[...], yn_ref[...]], axis=1)
    x1 = _layer_norm(alpha * x_ref[...] + _dot(y, wo_ref[...]), g1_ref[...], b1_ref[...])
    xb = x1.astype(BF16)
    acc_ref[...] = alpha * x1
    nchunk = wg_ref.shape[1] // FFN_CHUNK
    for c in range(nchunk):
        cols = slice(c * FFN_CHUNK, (c + 1) * FFN_CHUNK)
        gate = _dot(xb, wg_ref[:, cols])
        up = _dot(xb, wu_ref[:, cols])
        hid = (gate * _sigmoid(gate) * up).astype(BF16)
        acc_ref[...] += _dot(hid, wd_ref[cols, :])
    o_ref[...] = _layer_norm(acc_ref[...], g2_ref[...], b2_ref[...])


def _mix_ffn(x2, yl, yr, yn, wo_bf, g1, b1, wg_bf, wu_bf, wd_bf, g2, b2, layer, alpha, tm):
    n, d = x2.shape
    row = lambda width: pl.BlockSpec((tm, width), lambda i: (i, 0))
    const = lambda a: pl.BlockSpec((None,) + a.shape[1:], lambda i: (layer, 0, 0),
                                   pipeline_mode=pl.Buffered(1))
    return pl.pallas_call(
        functools.partial(_mix_ffn_kernel, alpha=alpha),
        grid=(n // tm,),
        in_specs=[row(d), row(yl.shape[1]), row(yr.shape[1]), row(yn.shape[1]),
                  const(wo_bf), const(g1), const(b1), const(wg_bf), const(wu_bf), const(wd_bf),
                  const(g2), const(b2)],
        out_specs=row(d),
        out_shape=jax.ShapeDtypeStruct((n, d), F32),
        scratch_shapes=[pltpu.VMEM((tm, d), F32)],
        compiler_params=_params("parallel"),
        name="mix_ffn_ln",
    )(x2, yl, yr, yn, wo_bf, g1, b1, wg_bf, wu_bf, wd_bf, g2, b2)


ROW_TILE = 1024
LRU_TIME_BLOCK = 256
RET_TIME_BLOCK = 2048
RET_BWD_TIME_BLOCK = 2048
NA_ROWS_PER_STEP = 32


def kernel(x, w_in, conv_w, conv_b, lru_w_a, lru_b_a, lru_w_x, lru_b_x, lru_lam, ret_gn_w, na_rpb,
           w_out, ln1_g, ln1_b, w_gate, w_up, w_down, ln2_g, ln2_b):
    nb, seq, d = x.shape
    depth = w_in.shape[0]
    alpha = (2 * depth) ** 0.25
    n = nb * seq
    tm = min(ROW_TILE, n)
    tabs = _ret_tables(seq)
    bias = _na_bias_table(na_rpb, seq // GRID_W)
    w_in_bf, w_out_bf = w_in.astype(BF16), w_out.astype(BF16)
    w_gate_bf, w_up_bf, w_down_bf = w_gate.astype(BF16), w_up.astype(BF16), w_down.astype(BF16)
    vec = lambda a: a.astype(F32)[:, None, :]
    ln = [vec(a) for a in (ln1_g, ln1_b, ln2_g, ln2_b)]
    x2 = x.reshape(n, d).astype(F32)
    for l in range(depth):
        lx2, lg2, rqkv2, rg2, na2 = _in_proj(x2, w_in_bf, l, tabs['cos'], tabs['sin'], tm)
        wg, ba, bx, nsp = _lru_weights(lru_w_a[l], lru_b_a[l], lru_w_x[l], lru_b_x[l], lru_lam[l])
        y_lru = _lru_mixer(lx2.reshape(nb, seq, -1), lg2.reshape(nb, seq, -1), conv_w[l].astype(F32),
                           conv_b[l].astype(F32)[None, :], wg, ba, bx, nsp, min(LRU_TIME_BLOCK, seq))
        y_ret = _ret_mixer(rqkv2.reshape(nb, seq, -1), rg2.reshape(nb, seq, -1), tabs,
                           ret_gn_w[l].astype(F32)[None, :], min(RET_TIME_BLOCK, seq),
                           min(RET_BWD_TIME_BLOCK, seq))
        y_na = _na_mixer(na2.reshape(nb, seq, -1), bias, l, NA_ROWS_PER_STEP)
        x2 = _mix_ffn(x2, y_lru.reshape(n, -1), y_ret.reshape(n, -1), y_na.reshape(n, -1),
                      w_out_bf, ln[0], ln[1], w_gate_bf, w_up_bf, w_down_bf, ln[2], ln[3], l, alpha, tm)
    return x2.reshape(nb, seq, d).astype(x.dtype)
```

```python
import functools
import math

import numpy as np
import jax
import jax.numpy as jnp
from jax import lax
from jax.experimental import pallas as pl
from jax.experimental.pallas import tpu as pltpu

F32 = jnp.float32
BF16 = jnp.bfloat16

LANES = 128
SUBLANES = 8
VMEM_LIMIT_BYTES = 56 * 1024 * 1024

GRID_W = 64
HEAD_DIM = 64
LRU_WIDTH = 384
RET_HEADS = 6
RET_WIDTH = 384
NA_HEADS = 4
NA_WIDTH = 256
CONV_WIDTH = 4
LRU_C = 8.0
ROPE_BASE = 10000.0
GN_EPS = 1e-6
NA_MAX_KH = 8
NA_KW = 16
LN_EPS = 1e-5
CHUNK = 128
HALO = SUBLANES

LOG2E = math.log2(math.e)
LRU_GROUPS = LRU_WIDTH // LANES
RET_GROUPS = RET_WIDTH // LANES


def _dot(a, b):
    return jnp.dot(a, b, preferred_element_type=F32)


def _dot_nt(a, b):
    return lax.dot_general(a, b, (((1,), (1,)), ((), ())), preferred_element_type=F32)


def _dot_tn(a, b):
    return lax.dot_general(a, b, (((0,), (0,)), ((), ())), preferred_element_type=F32)


def _params(*sem):
    return pltpu.CompilerParams(dimension_semantics=sem, vmem_limit_bytes=VMEM_LIMIT_BYTES)


def _sigmoid(z):
    return 1.0 / (1.0 + jnp.exp(-z))


def _in_proj_kernel(x_ref, w_ref, cos_ref, sin_ref, lx_ref, lg_ref, rqkv_ref, rg_ref, na_ref):
    xb = x_ref[...].astype(BF16)
    n_lru = 2 * LRU_WIDTH
    w = RET_WIDTH
    lru = _dot(xb, w_ref[:, :n_lru])
    lx_ref[...] = lru[:, :LRU_WIDTH]
    lg_ref[...] = jax.nn.gelu(lru[:, LRU_WIDTH:]).astype(lg_ref.dtype)
    ret = _dot(xb, w_ref[:, n_lru:n_lru + 4 * w])
    cos = cos_ref[...]
    sin = sin_ref[...]
    for g in range(2 * RET_GROUPS):
        lanes = slice(g * LANES, (g + 1) * LANES)
        t = _rope(ret[:, lanes], cos, sin)
        if g >= RET_GROUPS:
            t = t * (HEAD_DIM ** -0.5)
        rqkv_ref[:, lanes] = t.astype(rqkv_ref.dtype)
    rqkv_ref[:, 2 * w:] = ret[:, 2 * w:3 * w].astype(rqkv_ref.dtype)
    gate = ret[:, 3 * w:]
    rg_ref[...] = (gate * _sigmoid(gate)).astype(rg_ref.dtype)
    na_ref[...] = _dot(xb, w_ref[:, n_lru + 4 * w:]).astype(na_ref.dtype)


def _in_proj(x2, w_bf, layer, cos, sin, tm):
    n, d = x2.shape
    assert cos.shape[0] % tm == 0, "a row tile must not straddle two sequences (rotary position blocks)"
    seq_blocks = cos.shape[0] // tm
    n_na = 3 * NA_WIDTH
    row = lambda width: pl.BlockSpec((tm, width), lambda i: (i, 0))
    pos = pl.BlockSpec((tm, LANES), lambda i: (i % seq_blocks, 0))
    return pl.pallas_call(
        _in_proj_kernel,
        grid=(n // tm,),
        in_specs=[row(d), pl.BlockSpec((None,) + w_bf.shape[1:], lambda i: (layer, 0, 0),
                                       pipeline_mode=pl.Buffered(1)),
                  pos, pos],
        out_specs=[row(LRU_WIDTH), row(LRU_WIDTH), row(3 * RET_WIDTH), row(RET_WIDTH), row(n_na)],
        out_shape=[jax.ShapeDtypeStruct((n, LRU_WIDTH), F32),
                   jax.ShapeDtypeStruct((n, LRU_WIDTH), BF16),
                   jax.ShapeDtypeStruct((n, 3 * RET_WIDTH), BF16),
                   jax.ShapeDtypeStruct((n, RET_WIDTH), BF16),
                   jax.ShapeDtypeStruct((n, n_na), BF16)],
        compiler_params=_params("parallel"),
        name="in_proj",
    )(x2, w_bf, cos, sin)


LRU_ROW_TILE = 256
LRU_TILES_PER_BODY = 4


def _lru_gates(xc, wg_ref, ba_ref, bx_ref, nsp_ref, g):
    return _lru_gates_finish(xc, _dot(xc.astype(BF16), wg_ref[g]), ba_ref, bx_ref, nsp_ref, g)


def _lru_gates_finish(xc, z, ba_ref, bx_ref, nsp_ref, g):
    r = 1.0 / (1.0 + jnp.exp2(z[:, :LANES] + ba_ref[:, g * LANES:(g + 1) * LANES]))
    i = 1.0 / (1.0 + jnp.exp2(z[:, LANES:] + bx_ref[:, g * LANES:(g + 1) * LANES]))
    a = jnp.exp2(r * nsp_ref[:, g * LANES:(g + 1) * LANES])
    y = 1.0 - a * a
    root = jnp.where(y > 0.0, y * lax.rsqrt(y), 0.0)
    return a, root * (i * xc)


def _lru_fwd_na_kernel(x_ref, prev_ref, next_ref, cw_ref, cb_ref, wg_ref, ba_ref, bx_ref, nsp_ref,
                       q_ref, k_ref, v_ref, bias_ref, xc_ref, hf_ref, yna_ref,
                       xs_scr, a_scr, u_scr, h_scr, *, rows_n):
    i = pl.program_id(0)
    nblk = pl.num_programs(0)
    nb, tc, _ = x_ref.shape
    rows = tc * nb
    halo_rows = HALO * nb
    ntile = rows // LRU_ROW_TILE
    steps_per_tile = LRU_ROW_TILE // nb

    kh = min(NA_MAX_KH, rows_n)
    slab = kh * GRID_W
    na_rows = q_ref.shape[0] // GRID_W
    assert na_rows == (LRU_GROUPS + 1) * ntile, "one attention row per loop iteration"
    lane = lax.broadcasted_iota(jnp.int32, (GRID_W, NA_WIDTH), 1)
    zero = jnp.zeros((GRID_W, NA_WIDTH), q_ref.dtype)
    head_of_lane = [(lane >= h * HEAD_DIM) & (lane < (h + 1) * HEAD_DIM) for h in range(NA_HEADS)]
    first_row = (i % (rows_n // na_rows)) * na_rows

    def na_scores(rr):
        r = first_row + rr
        rstart = jnp.clip(r - kh // 2, 0, rows_n - kh)
        k0 = pl.multiple_of(rstart * GRID_W, GRID_W)
        q0 = pl.multiple_of(rr * GRID_W, GRID_W)
        q = q_ref[pl.ds(q0, GRID_W), :]
        q4 = jnp.concatenate([jnp.where(m, q, zero) for m in head_of_lane], axis=0)
        return q0, k0, _dot_nt(q4, k_ref[pl.ds(k0, slab), :]) + bias_ref[r - rstart]

    def na_softmax(s):
        e = jnp.exp2(s - jnp.max(s, axis=1, keepdims=True))
        return e.astype(BF16), jnp.sum(e, axis=1, keepdims=True)

    def na_output(q0, k0, p, l):
        o4 = _dot(p, v_ref[pl.ds(k0, slab), :]) / l
        o = o4[(NA_HEADS - 1) * GRID_W:]
        for h in reversed(range(NA_HEADS - 1)):
            o = jnp.where(lane < (h + 1) * HEAD_DIM, o4[h * GRID_W:(h + 1) * GRID_W], o)
        yna_ref[pl.ds(q0, GRID_W), :] = o.astype(yna_ref.dtype)

    @pl.when(i == 0)
    def _():
        h_scr[...] = jnp.zeros_like(h_scr)

    has_prev = (i > 0).astype(F32)
    has_next = (i < nblk - 1).astype(F32)
    for g in range(LRU_GROUPS):
        lanes = slice(g * LANES, (g + 1) * LANES)
        for b in range(nb):
            xs_scr[g, pl.ds(b, HALO, stride=nb), :] = prev_ref[b, :, lanes] * has_prev
            xs_scr[g, pl.ds(halo_rows + rows + b, HALO, stride=nb), :] = next_ref[b, :, lanes] * has_next
            if g == 0:
                xs_scr[g, pl.ds(halo_rows + b, tc, stride=nb), :] = x_ref[b, :, lanes]

    def to_time_major(k, g):
        lanes = slice(g * LANES, (g + 1) * LANES)
        t0 = pl.multiple_of(k * steps_per_tile, steps_per_tile)
        r0 = pl.multiple_of(k * LRU_ROW_TILE, LRU_ROW_TILE)
        for b in range(nb):
            xs_scr[g, pl.ds(halo_rows + r0 + b, steps_per_tile, stride=nb), :] = (
                x_ref[b, pl.ds(t0, steps_per_tile), lanes])

    left = CONV_WIDTH // 2

    def conv_and_gate_dot(k, g):
        lanes = slice(g * LANES, (g + 1) * LANES)
        r0 = pl.multiple_of(k * LRU_ROW_TILE, LRU_ROW_TILE)
        xc = jnp.broadcast_to(cb_ref[:, lanes], (LRU_ROW_TILE, LANES))
        for j in range(CONV_WIDTH):
            off = halo_rows + (j - left) * nb
            xc = xc + xs_scr[g, pl.ds(r0 + off, LRU_ROW_TILE), :] * cw_ref[j:j + 1, lanes]
        xc_ref[pl.ds(r0, LRU_ROW_TILE), lanes] = xc
        return xc, _dot(xc.astype(BF16), wg_ref[g])

    def coeffs_finish(k, g, xc, z):
        r0 = pl.multiple_of(k * LRU_ROW_TILE, LRU_ROW_TILE)
        a, u = _lru_gates_finish(xc, z, ba_ref, bx_ref, nsp_ref, g)
        a_scr[g, pl.ds(r0, LRU_ROW_TILE), :] = a
        u_scr[g, pl.ds(r0, LRU_ROW_TILE), :] = u

    def scan(k, g, h):
        lanes = slice(g * LANES, (g + 1) * LANES)
        r0 = pl.multiple_of(k * LRU_ROW_TILE, LRU_ROW_TILE)
        for t in range(steps_per_tile):
            r = r0 + t * nb
            h = a_scr[g, pl.ds(r, nb), :] * h + u_scr[g, pl.ds(r, nb), :]
            hf_ref[pl.ds(r, nb), lanes] = h
        return h

    per_body = LRU_TILES_PER_BODY
    for p in range(LRU_GROUPS + 1):
        def body(kb, h, p=p):
            ks = [kb * per_body + t for t in range(per_body)]
            gate_in = [conv_and_gate_dot(k, p) for k in ks] if p < LRU_GROUPS else []
            scores = [na_scores(p * ntile + k) for k in ks]
            for k, (xc, z) in zip(ks, gate_in):
                coeffs_finish(k, p, xc, z)
            probs = [na_softmax(s) for _, _, s in scores]
            for k in ks:
                if p >= 1:
                    h = scan(k, p - 1, h)
                if p + 1 < LRU_GROUPS:
                    to_time_major(k, p + 1)
            for (q0, k0, _), (pr, l) in zip(scores, probs):
                na_output(q0, k0, pr, l)
            return h

        h0 = h_scr[p - 1] if p >= 1 else jnp.zeros((nb, LANES), F32)
        h = lax.fori_loop(0, ntile // per_body, body, h0)
        if p >= 1:
            h_scr[p - 1] = h


def _lru_bwd_kernel(xc_ref, hf_ref, gate_ref, wg_ref, ba_ref, bx_ref, nsp_ref,
                    y_ref, a_scr, hb_scr, h_scr):
    i = pl.program_id(0)
    nb, tc, _ = gate_ref.shape
    rows = tc * nb
    ntile = rows // LRU_ROW_TILE
    steps_per_tile = LRU_ROW_TILE // nb

    @pl.when(i == 0)
    def _():
        h_scr[...] = jnp.zeros_like(h_scr)

    def coeffs(k, g):
        lanes = slice(g * LANES, (g + 1) * LANES)
        r0 = pl.multiple_of(k * LRU_ROW_TILE, LRU_ROW_TILE)
        a, u = _lru_gates(xc_ref[pl.ds(r0, LRU_ROW_TILE), lanes], wg_ref, ba_ref, bx_ref, nsp_ref, g)
        a_scr[g, pl.ds(r0, LRU_ROW_TILE), :] = a
        hb_scr[g, pl.ds(r0, LRU_ROW_TILE), :] = u

    def scan(k, g, h):
        lanes = slice(g * LANES, (g + 1) * LANES)
        r0 = pl.multiple_of(k * LRU_ROW_TILE, LRU_ROW_TILE)
        for t in reversed(range(steps_per_tile)):
            r = r0 + t * nb
            h = a_scr[g, pl.ds(r, nb), :] * h + hb_scr[g, pl.ds(r, nb), :]
            hb_scr[g, pl.ds(r, nb), :] = h + hf_ref[pl.ds(r, nb), lanes]
        return h

    def to_batch_major(k, g):
        lanes = slice(g * LANES, (g + 1) * LANES)
        t0 = pl.multiple_of(k * steps_per_tile, steps_per_tile)
        r0 = pl.multiple_of(k * LRU_ROW_TILE, LRU_ROW_TILE)
        for b in range(nb):
            hsum = hb_scr[g, pl.ds(r0 + b, steps_per_tile, stride=nb), :]
            gate = gate_ref[b, pl.ds(t0, steps_per_tile), lanes].astype(F32)
            y_ref[b, pl.ds(t0, steps_per_tile), lanes] = (hsum * gate).astype(y_ref.dtype)

    for p in range(LRU_GROUPS + 2):
        def body(kk, h, p=p):
            k = ntile - 1 - kk
            if p < LRU_GROUPS:
                coeffs(k, p)
            if 1 <= p <= LRU_GROUPS:
                h = scan(k, p - 1, h)
            if p >= 2:
                to_batch_major(k, p - 2)
            return h

        scanning = 1 <= p <= LRU_GROUPS
        h0 = h_scr[p - 1] if scanning else jnp.zeros((nb, LANES), F32)
        h = lax.fori_loop(0, ntile, body, h0, unroll=4)
        if scanning:
            h_scr[p - 1] = h


def _lru_na_mixer(lx3, lg3, cw, cb, wg, ba, bx, nsp, tc, na3, bias, layer):
    nb, seq, _ = lx3.shape
    assert nb == SUBLANES, "the RG-LRU kernels put the batch on the sublane axis"
    nblk = seq // tc
    rows = tc * nb
    hb = tc // HALO
    const2 = lambda i: (0, 0)
    const3 = lambda i: (0, 0, 0)
    rows_n = seq // GRID_W
    spb = nblk // nb
    tq = seq // spb
    xc_tm, hf_tm, y_na = pl.pallas_call(
        functools.partial(_lru_fwd_na_kernel, rows_n=rows_n),
        grid=(nblk,),
        in_specs=[pl.BlockSpec((nb, tc, LRU_WIDTH), lambda i: (0, i, 0)),
                  pl.BlockSpec((nb, HALO, LRU_WIDTH), lambda i: (0, jnp.maximum(i * hb - 1, 0), 0)),
                  pl.BlockSpec((nb, HALO, LRU_WIDTH),
                               lambda i: (0, jnp.minimum((i + 1) * hb, seq // HALO - 1), 0)),
                  pl.BlockSpec(cw.shape, const2), pl.BlockSpec(cb.shape, const2),
                  pl.BlockSpec(wg.shape[1:], const3), pl.BlockSpec(ba.shape[1:], const2),
                  pl.BlockSpec(bx.shape[1:], const2), pl.BlockSpec(nsp.shape[1:], const2),
                  pl.BlockSpec((None, tq, NA_WIDTH), lambda i: (i // spb, i % spb, 0)),
                  pl.BlockSpec((None, seq, NA_WIDTH), lambda i: (i // spb, 0, 1)),
                  pl.BlockSpec((None, seq, NA_WIDTH), lambda i: (i // spb, 0, 2)),
                  pl.BlockSpec((None,) + bias.shape[1:], lambda i: (layer, 0, 0, 0))],
        out_specs=[pl.BlockSpec((rows, LRU_WIDTH), lambda i: (i, 0)),
                   pl.BlockSpec((rows, LRU_WIDTH), lambda i: (i, 0)),
                   pl.BlockSpec((None, tq, NA_WIDTH), lambda i: (i // spb, i % spb, 0))],
        out_shape=[jax.ShapeDtypeStruct((seq * nb, LRU_WIDTH), F32),
                   jax.ShapeDtypeStruct((seq * nb, LRU_WIDTH), F32),
                   jax.ShapeDtypeStruct((nb, seq, NA_WIDTH), BF16)],
        scratch_shapes=[pltpu.VMEM((LRU_GROUPS, (tc + 2 * HALO) * nb, LANES), F32),
                        pltpu.VMEM((LRU_GROUPS, rows, LANES), F32),
                        pltpu.VMEM((LRU_GROUPS, rows, LANES), F32),
                        pltpu.VMEM((LRU_GROUPS, nb, LANES), F32)],
        compiler_params=_params("arbitrary"),
        name="lru_fwd_na",
    )(lx3, lx3, lx3, cw, cb, wg[0], ba[0], bx[0], nsp[0], na3, na3, na3, bias)

    rev = lambda i: (nblk - 1 - i, 0)
    y_lru = pl.pallas_call(
        _lru_bwd_kernel,
        grid=(nblk,),
        in_specs=[pl.BlockSpec((rows, LRU_WIDTH), rev),
                  pl.BlockSpec((rows, LRU_WIDTH), rev),
                  pl.BlockSpec((nb, tc, LRU_WIDTH), lambda i: (0, nblk - 1 - i, 0)),
                  pl.BlockSpec(wg.shape[1:], const3), pl.BlockSpec(ba.shape[1:], const2),
                  pl.BlockSpec(bx.shape[1:], const2), pl.BlockSpec(nsp.shape[1:], const2)],
        out_specs=pl.BlockSpec((nb, tc, LRU_WIDTH), lambda i: (0, nblk - 1 - i, 0)),
        out_shape=jax.ShapeDtypeStruct((nb, seq, LRU_WIDTH), BF16),
        scratch_shapes=[pltpu.VMEM((LRU_GROUPS, rows, LANES), F32),
                        pltpu.VMEM((LRU_GROUPS, rows, LANES), F32),
                        pltpu.VMEM((LRU_GROUPS, nb, LANES), F32)],
        compiler_params=_params("arbitrary"),
        name="lru_bwd",
    )(xc_tm, hf_tm, lg3, wg[1], ba[1], bx[1], nsp[1])
    return y_lru, y_na


def _lru_weights(w_a, b_a, w_x, b_x, lam):
    nblocks = w_a.shape[1]
    per_group = LANES // w_a.shape[2]

    def blockdiag(w):
        w = w.reshape(2, nblocks // per_group, per_group, w.shape[2], w.shape[3])
        eye = jnp.eye(per_group, dtype=w.dtype)
        w = w[:, :, :, :, None, :] * eye[None, None, :, None, :, None]
        return w.reshape(2, nblocks // per_group, LANES, LANES)

    wg = (jnp.concatenate([blockdiag(w_a), blockdiag(w_x)], axis=-1) * -LOG2E).astype(BF16)
    nsp = (-LRU_C * LOG2E) * jax.nn.softplus(-lam.astype(F32))
    return (wg, (b_a.astype(F32) * -LOG2E)[:, None, :], (b_x.astype(F32) * -LOG2E)[:, None, :],
            nsp[:, None, :])


def _ret_tables(seq):
    c = CHUNK
    log_g = jnp.log1p(-jnp.exp2(-5.0 - jnp.arange(RET_HEADS, dtype=F32)))
    lg_lane = jnp.repeat(log_g, HEAD_DIM)[None, :]
    idx = jnp.arange(c, dtype=F32)[:, None]
    t = {}
    t['q_f'] = jnp.exp((idx + 1.0) * lg_lane)
    t['q_b'] = jnp.exp((c - idx) * lg_lane)
    t['k_f'] = jnp.exp((c - 1 - idx) * lg_lane)
    t['k_b'] = jnp.exp(idx * lg_lane)
    dec = jnp.exp(jnp.abs(idx - idx.T)[None] * log_g[:, None, None])
    t['decay'] = dec.reshape(RET_GROUPS, 2, c, c).transpose(0, 2, 1, 3).reshape(RET_GROUPS, c, 2 * c)
    li = np.arange(LANES)
    same = jnp.asarray((li[:, None] // HEAD_DIM) == (li[None, :] // HEAD_DIM), F32)
    g_chunk = jnp.exp(c * log_g).reshape(RET_GROUPS, LANES // HEAD_DIM)
    t['gch'] = jnp.repeat(g_chunk, HEAD_DIM, axis=1)[:, :, None] * same[None]
    t['bd'] = same[None]
    half = HEAD_DIM // 2
    inv_freq = ROPE_BASE ** (-jnp.arange(half, dtype=F32) / half)
    ang = jnp.arange(seq, dtype=F32)[:, None] * inv_freq[None, :]
    cos, sin = jnp.cos(ang), jnp.sin(ang)
    reps = LANES // HEAD_DIM
    t['cos'] = jnp.tile(jnp.concatenate([cos, cos], axis=1), (1, reps))
    t['sin'] = jnp.tile(jnp.concatenate([-sin, sin], axis=1), (1, reps))
    return t


def _rope(t, cos, sin):
    half = HEAD_DIM // 2
    lane = lax.broadcasted_iota(jnp.int32, t.shape, 1)
    first = (lane % HEAD_DIM) < half
    partner = jnp.where(first, pltpu.roll(t, LANES - half, 1), pltpu.roll(t, half, 1))
    return t * cos + partner * sin


def _split_heads(t, lo):
    zero = jnp.zeros_like(t)
    return jnp.concatenate([jnp.where(lo, t, zero), jnp.where(lo, zero, t)], axis=0)


def _ret_blk(ref, c, g):
    return ref[c * CHUNK:(c + 1) * CHUNK, g * LANES:(g + 1) * LANES]


def _ret_bwd_kernel(k_ref, v_ref, kb_ref, gch_ref, bd_ref, st_ref, s_scr):
    j = pl.program_id(1)

    @pl.when(j == 0)
    def _():
        s_scr[...] = jnp.zeros_like(s_scr)

    nchunk = k_ref.shape[0] // CHUNK
    kv = {}
    for c in range(nchunk):
        for g in range(RET_GROUPS):
            kb = kb_ref[:, g * LANES:(g + 1) * LANES]
            kv[c, g] = _dot_tn((_ret_blk(k_ref, c, g).astype(F32) * kb).astype(BF16), _ret_blk(v_ref, c, g))
    for g in range(RET_GROUPS):
        st = s_scr[g]
        for c in reversed(range(nchunk)):
            st_ref[c, g] = st.astype(st_ref.dtype)
            st = gch_ref[g] * st + bd_ref[0] * kv[c, g]
        s_scr[g] = st


def _ret_fwd_kernel(q_ref, k_ref, v_ref, g_ref, qf_ref, qb_ref, kf_ref, dec_ref,
                    gch_ref, bd_ref, gnw_ref, stb_ref, y_ref, s_scr):
    j = pl.program_id(1)

    @pl.when(j == 0)
    def _():
        s_scr[...] = jnp.zeros_like(s_scr)

    nchunk = q_ref.shape[0] // CHUNK
    units = [(c, g) for c in range(nchunk) for g in range(RET_GROUPS)]
    lo = lax.broadcasted_iota(jnp.int32, (CHUNK, LANES), 1) < HEAD_DIM
    gn_avg = bd_ref[0].astype(BF16)

    s, kv = {}, {}
    for c, g in units:
        lanes = slice(g * LANES, (g + 1) * LANES)
        k = _ret_blk(k_ref, c, g)
        s[c, g] = _dot_nt(_ret_blk(q_ref, c, g), _split_heads(k.astype(F32), lo).astype(BF16))
        kv[c, g] = _dot_tn((k.astype(F32) * kf_ref[:, lanes]).astype(BF16), _ret_blk(v_ref, c, g))
    o = {}
    for c, g in units:
        v2 = _split_heads(_ret_blk(v_ref, c, g).astype(F32), lo).astype(BF16)
        o[c, g] = _dot((s[c, g] * dec_ref[g]).astype(BF16), v2)
    for g in range(RET_GROUPS):
        lanes = slice(g * LANES, (g + 1) * LANES)
        st = s_scr[g]
        for c in range(nchunk):
            q = _ret_blk(q_ref, c, g).astype(F32)
            q2 = jnp.concatenate([(q * qf_ref[:, lanes]).astype(BF16), (q * qb_ref[:, lanes]).astype(BF16)],
                                 axis=1)
            st2 = jnp.concatenate([st.astype(BF16), stb_ref[c, g]], axis=0)
            o[c, g] = o[c, g] + _dot(q2, st2)
            st = gch_ref[g] * st + bd_ref[0] * kv[c, g]
        s_scr[g] = st
    for g in range(RET_GROUPS):
        lanes = slice(g * LANES, (g + 1) * LANES)
        og = jnp.concatenate([o[c, g] for c in range(nchunk)], axis=0)
        oc = og - _dot(og.astype(BF16), gn_avg) * (1.0 / HEAD_DIM)
        var = _dot((oc * oc).astype(BF16), gn_avg) * (1.0 / HEAD_DIM)
        y = g_ref[:, lanes].astype(F32) * (oc * lax.rsqrt(var + GN_EPS) * gnw_ref[:, lanes])
        y_ref[:, lanes] = y.astype(y_ref.dtype)


def _ret_mixer(rqkv3, rg3, tabs, gn_w, tb, tb_bwd):
    nb, seq, _ = rqkv3.shape
    nblk = seq // tb
    cpb = tb // CHUNK
    w = RET_WIDTH
    const2 = lambda b, j: (0, 0)
    const3 = lambda b, j: (0, 0, 0)
    tab2 = lambda name: pl.BlockSpec(tabs[name].shape, const2)
    tab3 = lambda name: pl.BlockSpec(tabs[name].shape, const3)

    col = lambda cidx: pl.BlockSpec((None, tb, w), lambda b, j: (b, j, cidx))

    nblk_bwd = seq // tb_bwd
    rev = lambda cidx: pl.BlockSpec((None, tb_bwd, w), lambda b, j: (b, nblk_bwd - 1 - j, cidx))
    st_b = pl.pallas_call(
        _ret_bwd_kernel,
        grid=(nb, nblk_bwd),
        in_specs=[rev(1), rev(2), tab2('k_b'), tab3('gch'), tab3('bd')],
        out_specs=pl.BlockSpec((None, tb_bwd // CHUNK, RET_GROUPS, LANES, LANES),
                               lambda b, j: (b, nblk_bwd - 1 - j, 0, 0, 0)),
        out_shape=jax.ShapeDtypeStruct((nb, seq // CHUNK, RET_GROUPS, LANES, LANES), BF16),
        scratch_shapes=[pltpu.VMEM((RET_GROUPS, LANES, LANES), F32)],
        compiler_params=_params("parallel", "arbitrary"),
        name="ret_bwd",
    )(rqkv3, rqkv3, tabs['k_b'], tabs['gch'], tabs['bd'])

    return pl.pallas_call(
        _ret_fwd_kernel,
        grid=(nb, nblk),
        in_specs=[col(0), col(1), col(2), col(0),
                  tab2('q_f'), tab2('q_b'), tab2('k_f'), tab3('decay'), tab3('gch'), tab3('bd'),
                  pl.BlockSpec(gn_w.shape, const2),
                  pl.BlockSpec((None, cpb, RET_GROUPS, LANES, LANES), lambda b, j: (b, j, 0, 0, 0))],
        out_specs=pl.BlockSpec((None, tb, w), lambda b, j: (b, j, 0)),
        out_shape=jax.ShapeDtypeStruct((nb, seq, w), BF16),
        scratch_shapes=[pltpu.VMEM((RET_GROUPS, LANES, LANES), F32)],
        compiler_params=_params("parallel", "arbitrary"),
        name="ret_fwd",
    )(rqkv3, rqkv3, rqkv3, rg3, tabs['q_f'], tabs['q_b'], tabs['k_f'],
      tabs['decay'], tabs['gch'], tabs['bd'], gn_w, st_b)


def _na_bias_table(rpb, rows_n):
    kh = min(NA_MAX_KH, rows_n)
    var = np.arange(kh)
    j = np.arange(kh)
    dr = j[None, :] - var[:, None]
    ri = dr + NA_MAX_KH - 1
    c = np.arange(GRID_W)
    cstart = np.clip(c - NA_KW // 2, 0, GRID_W - NA_KW)
    kc = np.arange(GRID_W)
    dc = kc[None, :] - c[:, None]
    ci = np.clip(dc, 1 - NA_KW, NA_KW - 1) + NA_KW - 1
    valid = (kc[None, :] >= cstart[:, None]) & (kc[None, :] < cstart[:, None] + NA_KW)
    col_sel = np.zeros((GRID_W, GRID_W, rpb.shape[3]), np.float32)
    col_sel[c[:, None], kc[None, :], ci] = 1.0
    t = jnp.einsum('lhab,qkb->lhaqk', rpb.astype(F32), col_sel, precision=lax.Precision.HIGHEST)
    t = jnp.where(valid[None, None, None], t * LOG2E, -jnp.inf)
    b = jnp.stack([jnp.concatenate([t[:, :, ri[v, jj]] for jj in range(kh)], axis=-1) for v in range(kh)],
                  axis=1)
    return b.reshape(rpb.shape[0], kh, NA_HEADS * GRID_W, kh * GRID_W)


def _layer_norm(z, g, b):
    mu = jnp.mean(z, axis=-1, keepdims=True)
    zc = z - mu
    var = jnp.mean(zc * zc, axis=-1, keepdims=True)
    return zc * lax.rsqrt(var + LN_EPS) * g + b


FFN_CHUNK = 256


def _mix_ffn_kernel(x_ref, yl_ref, yr_ref, yn_ref, wo_ref, g1_ref, b1_ref, wg_ref, wu_ref, wd_ref,
                    g2_ref, b2_ref, o_ref, acc_ref, *, alpha):
    y = jnp.concatenate([yl_ref[...], yr_ref[...], yn_ref[...]], axis=1)
    x1 = _layer_norm(alpha * x_ref[...] + _dot(y, wo_ref[...]), g1_ref[...], b1_ref[...])
    xb = x1.astype(BF16)
    acc_ref[...] = alpha * x1
    nchunk = wg_ref.shape[1] // FFN_CHUNK
    for c in range(nchunk):
        cols = slice(c * FFN_CHUNK, (c + 1) * FFN_CHUNK)
        gate = _dot(xb, wg_ref[:, cols])
        up = _dot(xb, wu_ref[:, cols])
        hid = (gate * _sigmoid(gate) * up).astype(BF16)
        acc_ref[...] += _dot(hid, wd_ref[cols, :])
    o_ref[...] = _layer_norm(acc_ref[...], g2_ref[...], b2_ref[...])


def _mix_ffn(x2, yl, yr, yn, wo_bf, g1, b1, wg_bf, wu_bf, wd_bf, g2, b2, layer, alpha, tm):
    n, d = x2.shape
    row = lambda width: pl.BlockSpec((tm, width), lambda i: (i, 0))
    const = lambda a: pl.BlockSpec((None,) + a.shape[1:], lambda i: (layer, 0, 0),
                                   pipeline_mode=pl.Buffered(1))
    return pl.pallas_call(
        functools.partial(_mix_ffn_kernel, alpha=alpha),
        grid=(n // tm,),
        in_specs=[row(d), row(yl.shape[1]), row(yr.shape[1]), row(yn.shape[1]),
                  const(wo_bf), const(g1), const(b1), const(wg_bf), const(wu_bf), const(wd_bf),
                  const(g2), const(b2)],
        out_specs=row(d),
        out_shape=jax.ShapeDtypeStruct((n, d), F32),
        scratch_shapes=[pltpu.VMEM((tm, d), F32)],
        compiler_params=_params("parallel"),
        name="mix_ffn_ln",
    )(x2, yl, yr, yn, wo_bf, g1, b1, wg_bf, wu_bf, wd_bf, g2, b2)


ROW_TILE = 1024
LRU_TIME_BLOCK = 256
RET_TIME_BLOCK = 2048
RET_BWD_TIME_BLOCK = 2048


def kernel(x, w_in, conv_w, conv_b, lru_w_a, lru_b_a, lru_w_x, lru_b_x, lru_lam, ret_gn_w, na_rpb,
           w_out, ln1_g, ln1_b, w_gate, w_up, w_down, ln2_g, ln2_b):
    nb, seq, d = x.shape
    depth = w_in.shape[0]
    alpha = (2 * depth) ** 0.25
    n = nb * seq
    tm = min(ROW_TILE, n)
    tabs = _ret_tables(seq)
    bias = _na_bias_table(na_rpb, seq // GRID_W)
    na_q0 = 2 * LRU_WIDTH + 4 * RET_WIDTH
    col_scale = np.ones((w_in.shape[2],), np.float32)
    col_scale[na_q0:na_q0 + NA_WIDTH] = LOG2E * HEAD_DIM ** -0.5
    w_in_bf, w_out_bf = (w_in * col_scale).astype(BF16), w_out.astype(BF16)
    w_gate_bf, w_up_bf, w_down_bf = w_gate.astype(BF16), w_up.astype(BF16), w_down.astype(BF16)
    vec = lambda a: a.astype(F32)[:, None, :]
    ln = [vec(a) for a in (ln1_g, ln1_b, ln2_g, ln2_b)]
    x2 = x.reshape(n, d).astype(F32)
    for l in range(depth):
        lx2, lg2, rqkv2, rg2, na2 = _in_proj(x2, w_in_bf, l, tabs['cos'], tabs['sin'], tm)
        wg, ba, bx, nsp = _lru_weights(lru_w_a[l], lru_b_a[l], lru_w_x[l], lru_b_x[l], lru_lam[l])
        y_lru, y_na = _lru_na_mixer(lx2.reshape(nb, seq, -1), lg2.reshape(nb, seq, -1),
                                    conv_w[l].astype(F32), conv_b[l].astype(F32)[None, :], wg, ba, bx, nsp,
                                    min(LRU_TIME_BLOCK, seq), na2.reshape(nb, seq, -1), bias, l)
        y_ret = _ret_mixer(rqkv2.reshape(nb, seq, -1), rg2.reshape(nb, seq, -1), tabs,
                           ret_gn_w[l].astype(F32)[None, :], min(RET_TIME_BLOCK, seq),
                           min(RET_BWD_TIME_BLOCK, seq))
        x2 = _mix_ffn(x2, y_lru.reshape(n, -1), y_ret.reshape(n, -1), y_na.reshape(n, -1),
                      w_out_bf, ln[0], ln[1], w_gate_bf, w_up_bf, w_down_bf, ln[2], ln[3], l, alpha, tm)
    return x2.reshape(nb, seq, d).astype(x.dtype)
```

```python
import functools
import math

import numpy as np
import jax
import jax.numpy as jnp
from jax import lax
from jax.experimental import pallas as pl
from jax.experimental.pallas import tpu as pltpu

F32 = jnp.float32
BF16 = jnp.bfloat16

LANES = 128
SUBLANES = 8
VMEM_LIMIT_BYTES = 56 * 1024 * 1024

GRID_W = 64
HEAD_DIM = 64
LRU_WIDTH = 384
RET_HEADS = 6
RET_WIDTH = 384
NA_HEADS = 4
NA_WIDTH = 256
CONV_WIDTH = 4
LRU_C = 8.0
ROPE_BASE = 10000.0
GN_EPS = 1e-6
NA_MAX_KH = 8
NA_KW = 16
LN_EPS = 1e-5
CHUNK = 128
HALO = SUBLANES

LRU_GROUPS = LRU_WIDTH // LANES
RET_GROUPS = RET_WIDTH // LANES


def _dot(a, b):
    return jnp.dot(a, b, preferred_element_type=F32)


def _dot_nt(a, b):
    return lax.dot_general(a, b, (((1,), (1,)), ((), ())), preferred_element_type=F32)


def _dot_tn(a, b):
    return lax.dot_general(a, b, (((0,), (0,)), ((), ())), preferred_element_type=F32)


def _params(*sem):
    return pltpu.CompilerParams(dimension_semantics=sem, vmem_limit_bytes=VMEM_LIMIT_BYTES)


def _sigmoid(z):
    return 1.0 / (1.0 + jnp.exp(-z))


def _in_proj_kernel(x_ref, w_ref, cos_ref, sin_ref, lx_ref, lg_ref, rqkv_ref, rg_ref, na_ref):
    xb = x_ref[...].astype(BF16)
    n_lru = 2 * LRU_WIDTH
    w = RET_WIDTH
    lru = _dot(xb, w_ref[:, :n_lru])
    lx_ref[...] = lru[:, :LRU_WIDTH]
    lg_ref[...] = jax.nn.gelu(lru[:, LRU_WIDTH:]).astype(lg_ref.dtype)
    ret = _dot(xb, w_ref[:, n_lru:n_lru + 4 * w])
    cos = cos_ref[...]
    sin = sin_ref[...]
    for g in range(2 * RET_GROUPS):
        lanes = slice(g * LANES, (g + 1) * LANES)
        t = _rope(ret[:, lanes], cos, sin)
        if g >= RET_GROUPS:
            t = t * (HEAD_DIM ** -0.5)
        rqkv_ref[:, lanes] = t.astype(rqkv_ref.dtype)
    rqkv_ref[:, 2 * w:] = ret[:, 2 * w:3 * w].astype(rqkv_ref.dtype)
    gate = ret[:, 3 * w:]
    rg_ref[...] = (gate * _sigmoid(gate)).astype(rg_ref.dtype)
    na_ref[...] = _dot(xb, w_ref[:, n_lru + 4 * w:]).astype(na_ref.dtype)


def _in_proj(x2, w_bf, layer, cos, sin, tm):
    n, d = x2.shape
    assert cos.shape[0] % tm == 0, "a row tile must not straddle two sequences (rotary position blocks)"
    seq_blocks = cos.shape[0] // tm
    n_na = 3 * NA_WIDTH
    row = lambda width: pl.BlockSpec((tm, width), lambda i: (i, 0))
    pos = pl.BlockSpec((tm, LANES), lambda i: (i % seq_blocks, 0))
    return pl.pallas_call(
        _in_proj_kernel,
        grid=(n // tm,),
        in_specs=[row(d), pl.BlockSpec((None,) + w_bf.shape[1:], lambda i: (layer, 0, 0),
                                       pipeline_mode=pl.Buffered(1)),
                  pos, pos],
        out_specs=[row(LRU_WIDTH), row(LRU_WIDTH), row(3 * RET_WIDTH), row(RET_WIDTH), row(n_na)],
        out_shape=[jax.ShapeDtypeStruct((n, LRU_WIDTH), F32),
                   jax.ShapeDtypeStruct((n, LRU_WIDTH), BF16),
                   jax.ShapeDtypeStruct((n, 3 * RET_WIDTH), BF16),
                   jax.ShapeDtypeStruct((n, RET_WIDTH), BF16),
                   jax.ShapeDtypeStruct((n, n_na), BF16)],
        compiler_params=_params("parallel"),
        name="in_proj",
    )(x2, w_bf, cos, sin)


LRU_ROW_TILE = 256


def _lru_gates(xc, wg_ref, ba_ref, bx_ref, nsp_ref, g):
    z = _dot(xc.astype(BF16), wg_ref[g])
    r = 1.0 / (1.0 + jnp.exp2(z[:, :LANES] + ba_ref[:, g * LANES:(g + 1) * LANES]))
    i = 1.0 / (1.0 + jnp.exp2(z[:, LANES:] + bx_ref[:, g * LANES:(g + 1) * LANES]))
    a = jnp.exp2(r * nsp_ref[:, g * LANES:(g + 1) * LANES])
    y = 1.0 - a * a
    root = jnp.where(y > 0.0, y * lax.rsqrt(y), 0.0)
    return a, root * (i * xc)


def _lru_fwd_kernel(x_ref, prev_ref, next_ref, cw_ref, cb_ref, wg_ref, ba_ref, bx_ref, nsp_ref,
                    xc_ref, hf_ref, xs_scr, a_scr, u_scr, h_scr):
    i = pl.program_id(0)
    nblk = pl.num_programs(0)
    nb, tc, _ = x_ref.shape
    rows = tc * nb
    halo_rows = HALO * nb
    ntile = rows // LRU_ROW_TILE
    steps_per_tile = LRU_ROW_TILE // nb

    @pl.when(i == 0)
    def _():
        h_scr[...] = jnp.zeros_like(h_scr)

    has_prev = (i > 0).astype(F32)
    has_next = (i < nblk - 1).astype(F32)
    for g in range(LRU_GROUPS):
        lanes = slice(g * LANES, (g + 1) * LANES)
        for b in range(nb):
            xs_scr[g, pl.ds(b, HALO, stride=nb), :] = prev_ref[b, :, lanes] * has_prev
            xs_scr[g, pl.ds(halo_rows + rows + b, HALO, stride=nb), :] = next_ref[b, :, lanes] * has_next
            if g == 0:
                xs_scr[g, pl.ds(halo_rows + b, tc, stride=nb), :] = x_ref[b, :, lanes]

    def to_time_major(k, g):
        lanes = slice(g * LANES, (g + 1) * LANES)
        t0 = pl.multiple_of(k * steps_per_tile, steps_per_tile)
        r0 = pl.multiple_of(k * LRU_ROW_TILE, LRU_ROW_TILE)
        for b in range(nb):
            xs_scr[g, pl.ds(halo_rows + r0 + b, steps_per_tile, stride=nb), :] = (
                x_ref[b, pl.ds(t0, steps_per_tile), lanes])

    left = CONV_WIDTH // 2

    def coeffs(k, g):
        lanes = slice(g * LANES, (g + 1) * LANES)
        r0 = pl.multiple_of(k * LRU_ROW_TILE, LRU_ROW_TILE)
        xc = jnp.broadcast_to(cb_ref[:, lanes], (LRU_ROW_TILE, LANES))
        for j in range(CONV_WIDTH):
            off = halo_rows + (j - left) * nb
            xc = xc + xs_scr[g, pl.ds(r0 + off, LRU_ROW_TILE), :] * cw_ref[j:j + 1, lanes]
        a, u = _lru_gates(xc, wg_ref, ba_ref, bx_ref, nsp_ref, g)
        xc_ref[pl.ds(r0, LRU_ROW_TILE), lanes] = xc
        a_scr[g, pl.ds(r0, LRU_ROW_TILE), :] = a
        u_scr[g, pl.ds(r0, LRU_ROW_TILE), :] = u

    def scan(k, g, h):
        lanes = slice(g * LANES, (g + 1) * LANES)
        r0 = pl.multiple_of(k * LRU_ROW_TILE, LRU_ROW_TILE)
        for t in range(steps_per_tile):
            r = r0 + t * nb
            h = a_scr[g, pl.ds(r, nb), :] * h + u_scr[g, pl.ds(r, nb), :]
            hf_ref[pl.ds(r, nb), lanes] = h
        return h

    for p in range(LRU_GROUPS + 1):
        def body(k, h, p=p):
            if p < LRU_GROUPS:
                coeffs(k, p)
            if p >= 1:
                h = scan(k, p - 1, h)
            if p + 1 < LRU_GROUPS:
                to_time_major(k, p + 1)
            return h

        h0 = h_scr[p - 1] if p >= 1 else jnp.zeros((nb, LANES), F32)
        h = lax.fori_loop(0, ntile, body, h0, unroll=4)
        if p >= 1:
            h_scr[p - 1] = h


def _lru_bwd_kernel(xc_ref, hf_ref, gate_ref, wg_ref, ba_ref, bx_ref, nsp_ref,
                    rk_ref, rv_ref, kb_ref, gch_ref, bd_ref,
                    y_ref, st_ref, a_scr, hb_scr, h_scr, s_scr, *, ret_steps_per_batch):
    i = pl.program_id(0)
    nb, tc, _ = gate_ref.shape
    rows = tc * nb
    ntile = rows // LRU_ROW_TILE
    steps_per_tile = LRU_ROW_TILE // nb

    @pl.when(i == 0)
    def _():
        h_scr[...] = jnp.zeros_like(h_scr)

    _ret_bwd_body(i % ret_steps_per_batch == 0, rk_ref, rv_ref, kb_ref, gch_ref, bd_ref, st_ref, s_scr)

    def coeffs(k, g):
        lanes = slice(g * LANES, (g + 1) * LANES)
        r0 = pl.multiple_of(k * LRU_ROW_TILE, LRU_ROW_TILE)
        a, u = _lru_gates(xc_ref[pl.ds(r0, LRU_ROW_TILE), lanes], wg_ref, ba_ref, bx_ref, nsp_ref, g)
        a_scr[g, pl.ds(r0, LRU_ROW_TILE), :] = a
        hb_scr[g, pl.ds(r0, LRU_ROW_TILE), :] = u

    def scan(k, g, h):
        lanes = slice(g * LANES, (g + 1) * LANES)
        r0 = pl.multiple_of(k * LRU_ROW_TILE, LRU_ROW_TILE)
        for t in reversed(range(steps_per_tile)):
            r = r0 + t * nb
            h = a_scr[g, pl.ds(r, nb), :] * h + hb_scr[g, pl.ds(r, nb), :]
            hb_scr[g, pl.ds(r, nb), :] = h + hf_ref[pl.ds(r, nb), lanes]
        return h

    def to_batch_major(k, g):
        lanes = slice(g * LANES, (g + 1) * LANES)
        t0 = pl.multiple_of(k * steps_per_tile, steps_per_tile)
        r0 = pl.multiple_of(k * LRU_ROW_TILE, LRU_ROW_TILE)
        for b in range(nb):
            hsum = hb_scr[g, pl.ds(r0 + b, steps_per_tile, stride=nb), :]
            gate = gate_ref[b, pl.ds(t0, steps_per_tile), lanes].astype(F32)
            y_ref[b, pl.ds(t0, steps_per_tile), lanes] = (hsum * gate).astype(y_ref.dtype)

    for p in range(LRU_GROUPS + 2):
        def body(kk, h, p=p):
            k = ntile - 1 - kk
            if p < LRU_GROUPS:
                coeffs(k, p)
            if 1 <= p <= LRU_GROUPS:
                h = scan(k, p - 1, h)
            if p >= 2:
                to_batch_major(k, p - 2)
            return h

        scanning = 1 <= p <= LRU_GROUPS
        h0 = h_scr[p - 1] if scanning else jnp.zeros((nb, LANES), F32)
        h = lax.fori_loop(0, ntile, body, h0, unroll=4)
        if scanning:
            h_scr[p - 1] = h


def _lru_mixer(lx3, lg3, cw, cb, wg, ba, bx, nsp, tc, rqkv3, tabs):
    nb, seq, _ = lx3.shape
    assert nb == SUBLANES, "the RG-LRU kernels put the batch on the sublane axis"
    nblk = seq // tc
    rows = tc * nb
    hb = tc // HALO
    const2 = lambda i: (0, 0)
    const3 = lambda i: (0, 0, 0)
    xc_tm, hf_tm = pl.pallas_call(
        _lru_fwd_kernel,
        grid=(nblk,),
        in_specs=[pl.BlockSpec((nb, tc, LRU_WIDTH), lambda i: (0, i, 0)),
                  pl.BlockSpec((nb, HALO, LRU_WIDTH), lambda i: (0, jnp.maximum(i * hb - 1, 0), 0)),
                  pl.BlockSpec((nb, HALO, LRU_WIDTH),
                               lambda i: (0, jnp.minimum((i + 1) * hb, seq // HALO - 1), 0)),
                  pl.BlockSpec(cw.shape, const2), pl.BlockSpec(cb.shape, const2),
                  pl.BlockSpec(wg.shape[1:], const3), pl.BlockSpec(ba.shape[1:], const2),
                  pl.BlockSpec(bx.shape[1:], const2), pl.BlockSpec(nsp.shape[1:], const2)],
        out_specs=[pl.BlockSpec((rows, LRU_WIDTH), lambda i: (i, 0)),
                   pl.BlockSpec((rows, LRU_WIDTH), lambda i: (i, 0))],
        out_shape=[jax.ShapeDtypeStruct((seq * nb, LRU_WIDTH), F32),
                   jax.ShapeDtypeStruct((seq * nb, LRU_WIDTH), F32)],
        scratch_shapes=[pltpu.VMEM((LRU_GROUPS, (tc + 2 * HALO) * nb, LANES), F32),
                        pltpu.VMEM((LRU_GROUPS, rows, LANES), F32),
                        pltpu.VMEM((LRU_GROUPS, rows, LANES), F32),
                        pltpu.VMEM((LRU_GROUPS, nb, LANES), F32)],
        compiler_params=_params("arbitrary"),
        name="lru_fwd",
    )(lx3, lx3, lx3, cw, cb, wg[0], ba[0], bx[0], nsp[0])

    rev = lambda i: (nblk - 1 - i, 0)
    spb = nblk // nb
    tr = seq // spb
    ret_col = lambda cidx: pl.BlockSpec((None, tr, RET_WIDTH), lambda i: (i // spb, spb - 1 - i % spb, cidx))
    tab = lambda name: pl.BlockSpec(tabs[name].shape, lambda i: (0,) * tabs[name].ndim)
    return pl.pallas_call(
        functools.partial(_lru_bwd_kernel, ret_steps_per_batch=spb),
        grid=(nblk,),
        in_specs=[pl.BlockSpec((rows, LRU_WIDTH), rev),
                  pl.BlockSpec((rows, LRU_WIDTH), rev),
                  pl.BlockSpec((nb, tc, LRU_WIDTH), lambda i: (0, nblk - 1 - i, 0)),
                  pl.BlockSpec(wg.shape[1:], const3), pl.BlockSpec(ba.shape[1:], const2),
                  pl.BlockSpec(bx.shape[1:], const2), pl.BlockSpec(nsp.shape[1:], const2),
                  ret_col(1), ret_col(2), tab('k_b'), tab('gch'), tab('bd')],
        out_specs=[pl.BlockSpec((nb, tc, LRU_WIDTH), lambda i: (0, nblk - 1 - i, 0)),
                   pl.BlockSpec((None, tr // CHUNK, RET_GROUPS, LANES, LANES),
                                lambda i: (i // spb, spb - 1 - i % spb, 0, 0, 0))],
        out_shape=[jax.ShapeDtypeStruct((nb, seq, LRU_WIDTH), BF16),
                   jax.ShapeDtypeStruct((nb, seq // CHUNK, RET_GROUPS, LANES, LANES), BF16)],
        scratch_shapes=[pltpu.VMEM((LRU_GROUPS, rows, LANES), F32),
                        pltpu.VMEM((LRU_GROUPS, rows, LANES), F32),
                        pltpu.VMEM((LRU_GROUPS, nb, LANES), F32),
                        pltpu.VMEM((RET_GROUPS, LANES, LANES), F32)],
        compiler_params=_params("arbitrary"),
        name="lru_bwd_ret_bwd",
    )(xc_tm, hf_tm, lg3, wg[1], ba[1], bx[1], nsp[1], rqkv3, rqkv3, tabs['k_b'], tabs['gch'], tabs['bd'])


def _lru_weights(w_a, b_a, w_x, b_x, lam):
    nblocks = w_a.shape[1]
    per_group = LANES // w_a.shape[2]

    def blockdiag(w):
        w = w.reshape(2, nblocks // per_group, per_group, w.shape[2], w.shape[3])
        eye = jnp.eye(per_group, dtype=w.dtype)
        w = w[:, :, :, :, None, :] * eye[None, None, :, None, :, None]
        return w.reshape(2, nblocks // per_group, LANES, LANES)

    log2e = math.log2(math.e)
    wg = (jnp.concatenate([blockdiag(w_a), blockdiag(w_x)], axis=-1) * -log2e).astype(BF16)
    nsp = (-LRU_C * log2e) * jax.nn.softplus(-lam.astype(F32))
    return (wg, (b_a.astype(F32) * -log2e)[:, None, :], (b_x.astype(F32) * -log2e)[:, None, :],
            nsp[:, None, :])


def _ret_tables(seq):
    c = CHUNK
    log_g = jnp.log1p(-jnp.exp2(-5.0 - jnp.arange(RET_HEADS, dtype=F32)))
    lg_lane = jnp.repeat(log_g, HEAD_DIM)[None, :]
    idx = jnp.arange(c, dtype=F32)[:, None]
    t = {}
    t['q_f'] = jnp.exp((idx + 1.0) * lg_lane)
    t['q_b'] = jnp.exp((c - idx) * lg_lane)
    t['k_f'] = jnp.exp((c - 1 - idx) * lg_lane)
    t['k_b'] = jnp.exp(idx * lg_lane)
    dec = jnp.exp(jnp.abs(idx - idx.T)[None] * log_g[:, None, None])
    t['decay'] = dec.reshape(RET_GROUPS, 2, c, c).transpose(0, 2, 1, 3).reshape(RET_GROUPS, c, 2 * c)
    li = np.arange(LANES)
    same = jnp.asarray((li[:, None] // HEAD_DIM) == (li[None, :] // HEAD_DIM), F32)
    g_chunk = jnp.exp(c * log_g).reshape(RET_GROUPS, LANES // HEAD_DIM)
    t['gch'] = jnp.repeat(g_chunk, HEAD_DIM, axis=1)[:, :, None] * same[None]
    t['bd'] = same[None]
    half = HEAD_DIM // 2
    inv_freq = ROPE_BASE ** (-jnp.arange(half, dtype=F32) / half)
    ang = jnp.arange(seq, dtype=F32)[:, None] * inv_freq[None, :]
    cos, sin = jnp.cos(ang), jnp.sin(ang)
    reps = LANES // HEAD_DIM
    t['cos'] = jnp.tile(jnp.concatenate([cos, cos], axis=1), (1, reps))
    t['sin'] = jnp.tile(jnp.concatenate([-sin, sin], axis=1), (1, reps))
    return t


def _rope(t, cos, sin):
    half = HEAD_DIM // 2
    lane = lax.broadcasted_iota(jnp.int32, t.shape, 1)
    first = (lane % HEAD_DIM) < half
    partner = jnp.where(first, pltpu.roll(t, LANES - half, 1), pltpu.roll(t, half, 1))
    return t * cos + partner * sin


def _split_heads(t, lo):
    zero = jnp.zeros_like(t)
    return jnp.concatenate([jnp.where(lo, t, zero), jnp.where(lo, zero, t)], axis=0)


def _ret_blk(ref, c, g):
    return ref[c * CHUNK:(c + 1) * CHUNK, g * LANES:(g + 1) * LANES]


def _ret_bwd_body(first, k_ref, v_ref, kb_ref, gch_ref, bd_ref, st_ref, s_scr):
    @pl.when(first)
    def _():
        s_scr[...] = jnp.zeros_like(s_scr)

    nchunk = k_ref.shape[0] // CHUNK
    kv = {}
    for c in range(nchunk):
        for g in range(RET_GROUPS):
            kb = kb_ref[:, g * LANES:(g + 1) * LANES]
            kv[c, g] = _dot_tn((_ret_blk(k_ref, c, g).astype(F32) * kb).astype(BF16), _ret_blk(v_ref, c, g))
    for g in range(RET_GROUPS):
        st = s_scr[g]
        for c in reversed(range(nchunk)):
            st_ref[c, g] = st.astype(st_ref.dtype)
            st = gch_ref[g] * st + bd_ref[0] * kv[c, g]
        s_scr[g] = st


def _ret_fwd_kernel(q_ref, k_ref, v_ref, g_ref, qf_ref, qb_ref, kf_ref, dec_ref,
                    gch_ref, bd_ref, gnw_ref, stb_ref, y_ref, s_scr):
    j = pl.program_id(1)

    @pl.when(j == 0)
    def _():
        s_scr[...] = jnp.zeros_like(s_scr)

    nchunk = q_ref.shape[0] // CHUNK
    units = [(c, g) for c in range(nchunk) for g in range(RET_GROUPS)]
    lo = lax.broadcasted_iota(jnp.int32, (CHUNK, LANES), 1) < HEAD_DIM
    gn_avg = bd_ref[0].astype(BF16)

    s, kv = {}, {}
    for c, g in units:
        lanes = slice(g * LANES, (g + 1) * LANES)
        k = _ret_blk(k_ref, c, g)
        s[c, g] = _dot_nt(_ret_blk(q_ref, c, g), _split_heads(k.astype(F32), lo).astype(BF16))
        kv[c, g] = _dot_tn((k.astype(F32) * kf_ref[:, lanes]).astype(BF16), _ret_blk(v_ref, c, g))
    o = {}
    for c, g in units:
        v2 = _split_heads(_ret_blk(v_ref, c, g).astype(F32), lo).astype(BF16)
        o[c, g] = _dot((s[c, g] * dec_ref[g]).astype(BF16), v2)
    for g in range(RET_GROUPS):
        lanes = slice(g * LANES, (g + 1) * LANES)
        st = s_scr[g]
        for c in range(nchunk):
            q = _ret_blk(q_ref, c, g).astype(F32)
            q2 = jnp.concatenate([(q * qf_ref[:, lanes]).astype(BF16), (q * qb_ref[:, lanes]).astype(BF16)],
                                 axis=1)
            st2 = jnp.concatenate([st.astype(BF16), stb_ref[c, g]], axis=0)
            o[c, g] = o[c, g] + _dot(q2, st2)
            st = gch_ref[g] * st + bd_ref[0] * kv[c, g]
        s_scr[g] = st
    for g in range(RET_GROUPS):
        lanes = slice(g * LANES, (g + 1) * LANES)
        og = jnp.concatenate([o[c, g] for c in range(nchunk)], axis=0)
        oc = og - _dot(og.astype(BF16), gn_avg) * (1.0 / HEAD_DIM)
        var = _dot((oc * oc).astype(BF16), gn_avg) * (1.0 / HEAD_DIM)
        y = g_ref[:, lanes].astype(F32) * (oc * lax.rsqrt(var + GN_EPS) * gnw_ref[:, lanes])
        y_ref[:, lanes] = y.astype(y_ref.dtype)


def _ret_mixer(rqkv3, rg3, st_b, tabs, gn_w, tb):
    nb, seq, _ = rqkv3.shape
    nblk = seq // tb
    cpb = tb // CHUNK
    w = RET_WIDTH
    const2 = lambda b, j: (0, 0)
    const3 = lambda b, j: (0, 0, 0)
    tab2 = lambda name: pl.BlockSpec(tabs[name].shape, const2)
    tab3 = lambda name: pl.BlockSpec(tabs[name].shape, const3)

    col = lambda cidx: pl.BlockSpec((None, tb, w), lambda b, j: (b, j, cidx))

    return pl.pallas_call(
        _ret_fwd_kernel,
        grid=(nb, nblk),
        in_specs=[col(0), col(1), col(2), col(0),
                  tab2('q_f'), tab2('q_b'), tab2('k_f'), tab3('decay'), tab3('gch'), tab3('bd'),
                  pl.BlockSpec(gn_w.shape, const2),
                  pl.BlockSpec((None, cpb, RET_GROUPS, LANES, LANES), lambda b, j: (b, j, 0, 0, 0))],
        out_specs=pl.BlockSpec((None, tb, w), lambda b, j: (b, j, 0)),
        out_shape=jax.ShapeDtypeStruct((nb, seq, w), BF16),
        scratch_shapes=[pltpu.VMEM((RET_GROUPS, LANES, LANES), F32)],
        compiler_params=_params("parallel", "arbitrary"),
        name="ret_fwd",
    )(rqkv3, rqkv3, rqkv3, rg3, tabs['q_f'], tabs['q_b'], tabs['k_f'],
      tabs['decay'], tabs['gch'], tabs['bd'], gn_w, st_b)


def _na_bias_table(rpb, rows_n):
    kh = min(NA_MAX_KH, rows_n)
    var = np.arange(kh)
    j = np.arange(kh)
    dr = j[None, :] - var[:, None]
    ri = dr + NA_MAX_KH - 1
    c = np.arange(GRID_W)
    cstart = np.clip(c - NA_KW // 2, 0, GRID_W - NA_KW)
    kc = np.arange(GRID_W)
    dc = kc[None, :] - c[:, None]
    ci = np.clip(dc, 1 - NA_KW, NA_KW - 1) + NA_KW - 1
    valid = (kc[None, :] >= cstart[:, None]) & (kc[None, :] < cstart[:, None] + NA_KW)
    col_sel = np.zeros((GRID_W, GRID_W, rpb.shape[3]), np.float32)
    col_sel[c[:, None], kc[None, :], ci] = 1.0
    t = jnp.einsum('lhab,qkb->lhaqk', rpb.astype(F32), col_sel, precision=lax.Precision.HIGHEST)
    t = jnp.where(valid[None, None, None], t, -jnp.inf)
    b = jnp.stack([jnp.concatenate([t[:, :, ri[v, jj]] for jj in range(kh)], axis=-1) for v in range(kh)],
                  axis=1)
    return b.reshape(rpb.shape[0], kh, NA_HEADS * GRID_W, kh * GRID_W)


def _na_kernel(q_ref, k_ref, v_ref, bias_ref, y_ref, *, rows_per_step, rows_n):
    jb = pl.program_id(1)
    kh = min(NA_MAX_KH, rows_n)
    slab = kh * GRID_W
    lane = lax.broadcasted_iota(jnp.int32, (GRID_W, NA_WIDTH), 1)
    zero = jnp.zeros((GRID_W, NA_WIDTH), q_ref.dtype)
    head_of_lane = [(lane >= h * HEAD_DIM) & (lane < (h + 1) * HEAD_DIM) for h in range(NA_HEADS)]

    k0, s = [], []
    for rr in range(rows_per_step):
        r = jb * rows_per_step + rr
        rstart = jnp.clip(r - kh // 2, 0, rows_n - kh)
        k0.append(pl.multiple_of(rstart * GRID_W, GRID_W))
        q = q_ref[rr * GRID_W:(rr + 1) * GRID_W, :] * (HEAD_DIM ** -0.5)
        q4 = jnp.concatenate([jnp.where(m, q, zero) for m in head_of_lane], axis=0)
        s.append(_dot_nt(q4, k_ref[pl.ds(k0[rr], slab), :]) + bias_ref[r - rstart])
    p, l = [], []
    for rr in range(rows_per_step):
        e = jnp.exp(s[rr] - jnp.max(s[rr], axis=1, keepdims=True))
        l.append(jnp.sum(e, axis=1, keepdims=True))
        p.append(e.astype(BF16))
    for rr in range(rows_per_step):
        o4 = _dot(p[rr], v_ref[pl.ds(k0[rr], slab), :]) / l[rr]
        o = o4[(NA_HEADS - 1) * GRID_W:]
        for h in reversed(range(NA_HEADS - 1)):
            o = jnp.where(lane < (h + 1) * HEAD_DIM, o4[h * GRID_W:(h + 1) * GRID_W], o)
        y_ref[rr * GRID_W:(rr + 1) * GRID_W, :] = o.astype(y_ref.dtype)


def _na_mixer(na3, bias, layer, rows_per_step):
    nb, seq, _ = na3.shape
    rows_n = seq // GRID_W
    tq = rows_per_step * GRID_W
    kern = functools.partial(_na_kernel, rows_per_step=rows_per_step, rows_n=rows_n)
    return pl.pallas_call(
        kern,
        grid=(nb, rows_n // rows_per_step),
        in_specs=[pl.BlockSpec((None, tq, NA_WIDTH), lambda b, j: (b, j, 0)),
                  pl.BlockSpec((None, seq, NA_WIDTH), lambda b, j: (b, 0, 1)),
                  pl.BlockSpec((None, seq, NA_WIDTH), lambda b, j: (b, 0, 2)),
                  pl.BlockSpec((None,) + bias.shape[1:], lambda b, j: (layer, 0, 0, 0))],
        out_specs=pl.BlockSpec((None, tq, NA_WIDTH), lambda b, j: (b, j, 0)),
        out_shape=jax.ShapeDtypeStruct((nb, seq, NA_WIDTH), BF16),
        compiler_params=_params("parallel", "arbitrary"),
        name="na",
    )(na3, na3, na3, bias)


def _layer_norm(z, g, b):
    mu = jnp.mean(z, axis=-1, keepdims=True)
    zc = z - mu
    var = jnp.mean(zc * zc, axis=-1, keepdims=True)
    return zc * lax.rsqrt(var + LN_EPS) * g + b


FFN_CHUNK = 256


def _mix_ffn_kernel(x_ref, yl_ref, yr_ref, yn_ref, wo_ref, g1_ref, b1_ref, wg_ref, wu_ref, wd_ref,
                    g2_ref, b2_ref, o_ref, acc_ref, *, alpha):
    y = jnp.concatenate([yl_ref[...], yr_ref[...], yn_ref[...]], axis=1)
    x1 = _layer_norm(alpha * x_ref[...] + _dot(y, wo_ref[...]), g1_ref[...], b1_ref[...])
    xb = x1.astype(BF16)
    acc_ref[...] = alpha * x1
    nchunk = wg_ref.shape[1] // FFN_CHUNK
    for c in range(nchunk):
        cols = slice(c * FFN_CHUNK, (c + 1) * FFN_CHUNK)
        gate = _dot(xb, wg_ref[:, cols])
        up = _dot(xb, wu_ref[:, cols])
        hid = (gate * _sigmoid(gate) * up).astype(BF16)
        acc_ref[...] += _dot(hid, wd_ref[cols, :])
    o_ref[...] = _layer_norm(acc_ref[...], g2_ref[...], b2_ref[...])


def _mix_ffn(x2, yl, yr, yn, wo_bf, g1, b1, wg_bf, wu_bf, wd_bf, g2, b2, layer, alpha, tm):
    n, d = x2.shape
    row = lambda width: pl.BlockSpec((tm, width), lambda i: (i, 0))
    const = lambda a: pl.BlockSpec((None,) + a.shape[1:], lambda i: (layer, 0, 0),
                                   pipeline_mode=pl.Buffered(1))
    return pl.pallas_call(
        functools.partial(_mix_ffn_kernel, alpha=alpha),
        grid=(n // tm,),
        in_specs=[row(d), row(yl.shape[1]), row(yr.shape[1]), row(yn.shape[1]),
                  const(wo_bf), const(g1), const(b1), const(wg_bf), const(wu_bf), const(wd_bf),
                  const(g2), const(b2)],
        out_specs=row(d),
        out_shape=jax.ShapeDtypeStruct((n, d), F32),
        scratch_shapes=[pltpu.VMEM((tm, d), F32)],
        compiler_params=_params("parallel"),
        name="mix_ffn_ln",
    )(x2, yl, yr, yn, wo_bf, g1, b1, wg_bf, wu_bf, wd_bf, g2, b2)


ROW_TILE = 1024
LRU_TIME_BLOCK = 256
RET_TIME_BLOCK = 2048
NA_ROWS_PER_STEP = 32


def kernel(x, w_in, conv_w, conv_b, lru_w_a, lru_b_a, lru_w_x, lru_b_x, lru_lam, ret_gn_w, na_rpb,
           w_out, ln1_g, ln1_b, w_gate, w_up, w_down, ln2_g, ln2_b):
    nb, seq, d = x.shape
    depth = w_in.shape[0]
    alpha = (2 * depth) ** 0.25
    n = nb * seq
    tm = min(ROW_TILE, n)
    tabs = _ret_tables(seq)
    bias = _na_bias_table(na_rpb, seq // GRID_W)
    w_in_bf, w_out_bf = w_in.astype(BF16), w_out.astype(BF16)
    w_gate_bf, w_up_bf, w_down_bf = w_gate.astype(BF16), w_up.astype(BF16), w_down.astype(BF16)
    vec = lambda a: a.astype(F32)[:, None, :]
    ln = [vec(a) for a in (ln1_g, ln1_b, ln2_g, ln2_b)]
    x2 = x.reshape(n, d).astype(F32)
    for l in range(depth):
        lx2, lg2, rqkv2, rg2, na2 = _in_proj(x2, w_in_bf, l, tabs['cos'], tabs['sin'], tm)
        wg, ba, bx, nsp = _lru_weights(lru_w_a[l], lru_b_a[l], lru_w_x[l], lru_b_x[l], lru_lam[l])
        rqkv3 = rqkv2.reshape(nb, seq, -1)
        y_lru, st_b = _lru_mixer(lx2.reshape(nb, seq, -1), lg2.reshape(nb, seq, -1), conv_w[l].astype(F32),
                                 conv_b[l].astype(F32)[None, :], wg, ba, bx, nsp, min(LRU_TIME_BLOCK, seq),
                                 rqkv3, tabs)
        y_ret = _ret_mixer(rqkv3, rg2.reshape(nb, seq, -1), st_b, tabs,
                           ret_gn_w[l].astype(F32)[None, :], min(RET_TIME_BLOCK, seq))
        y_na = _na_mixer(na2.reshape(nb, seq, -1), bias, l, NA_ROWS_PER_STEP)
        x2 = _mix_ffn(x2, y_lru.reshape(n, -1), y_ret.reshape(n, -1), y_na.reshape(n, -1),
                      w_out_bf, ln[0], ln[1], w_gate_bf, w_up_bf, w_down_bf, ln[2], ln[3], l, alpha, tm)
    return x2.reshape(nb, seq, d).astype(x.dtype)
```

```python
import functools
import math

import numpy as np
import jax
import jax.numpy as jnp
from jax import lax
from jax.experimental import pallas as pl
from jax.experimental.pallas import tpu as pltpu

F32 = jnp.float32
BF16 = jnp.bfloat16

LANES = 128
SUBLANES = 8
VMEM_LIMIT_BYTES = 56 * 1024 * 1024

GRID_W = 64
HEAD_DIM = 64
LRU_WIDTH = 384
RET_HEADS = 6
RET_WIDTH = 384
NA_HEADS = 4
NA_WIDTH = 256
CONV_WIDTH = 4
LRU_C = 8.0
ROPE_BASE = 10000.0
GN_EPS = 1e-6
NA_MAX_KH = 8
NA_KW = 16
LN_EPS = 1e-5
CHUNK = 128
HALO = SUBLANES

LRU_GROUPS = LRU_WIDTH // LANES
RET_GROUPS = RET_WIDTH // LANES


def _dot(a, b):
    return jnp.dot(a, b, preferred_element_type=F32)


def _dot_nt(a, b):
    return lax.dot_general(a, b, (((1,), (1,)), ((), ())), preferred_element_type=F32)


def _dot_tn(a, b):
    return lax.dot_general(a, b, (((0,), (0,)), ((), ())), preferred_element_type=F32)


def _params(*sem):
    return pltpu.CompilerParams(dimension_semantics=sem, vmem_limit_bytes=VMEM_LIMIT_BYTES)


def _sigmoid(z):
    return 1.0 / (1.0 + jnp.exp(-z))


def _in_proj_kernel(x_ref, w_ref, cos_ref, sin_ref, lx_ref, lg_ref, rqkv_ref, rg_ref, na_ref):
    xb = x_ref[...].astype(BF16)
    n_lru = 2 * LRU_WIDTH
    w = RET_WIDTH
    lru = _dot(xb, w_ref[:, :n_lru])
    lx_ref[...] = lru[:, :LRU_WIDTH]
    lg_ref[...] = jax.nn.gelu(lru[:, LRU_WIDTH:]).astype(lg_ref.dtype)
    ret = _dot(xb, w_ref[:, n_lru:n_lru + 4 * w])
    cos = cos_ref[...]
    sin = sin_ref[...]
    for g in range(2 * RET_GROUPS):
        lanes = slice(g * LANES, (g + 1) * LANES)
        t = _rope(ret[:, lanes], cos, sin)
        if g >= RET_GROUPS:
            t = t * (HEAD_DIM ** -0.5)
        rqkv_ref[:, lanes] = t.astype(rqkv_ref.dtype)
    rqkv_ref[:, 2 * w:] = ret[:, 2 * w:3 * w].astype(rqkv_ref.dtype)
    gate = ret[:, 3 * w:]
    rg_ref[...] = (gate * _sigmoid(gate)).astype(rg_ref.dtype)
    na_ref[...] = _dot(xb, w_ref[:, n_lru + 4 * w:]).astype(na_ref.dtype)


def _in_proj(x2, w_bf, layer, cos, sin, tm):
    n, d = x2.shape
    assert cos.shape[0] % tm == 0, "a row tile must not straddle two sequences (rotary position blocks)"
    seq_blocks = cos.shape[0] // tm
    n_na = 3 * NA_WIDTH
    row = lambda width: pl.BlockSpec((tm, width), lambda i: (i, 0))
    pos = pl.BlockSpec((tm, LANES), lambda i: (i % seq_blocks, 0))
    return pl.pallas_call(
        _in_proj_kernel,
        grid=(n // tm,),
        in_specs=[row(d), pl.BlockSpec((None,) + w_bf.shape[1:], lambda i: (layer, 0, 0),
                                       pipeline_mode=pl.Buffered(1)),
                  pos, pos],
        out_specs=[row(LRU_WIDTH), row(LRU_WIDTH), row(3 * RET_WIDTH), row(RET_WIDTH), row(n_na)],
        out_shape=[jax.ShapeDtypeStruct((n, LRU_WIDTH), F32),
                   jax.ShapeDtypeStruct((n, LRU_WIDTH), BF16),
                   jax.ShapeDtypeStruct((n, 3 * RET_WIDTH), BF16),
                   jax.ShapeDtypeStruct((n, RET_WIDTH), BF16),
                   jax.ShapeDtypeStruct((n, n_na), BF16)],
        compiler_params=_params("parallel"),
        name="in_proj",
    )(x2, w_bf, cos, sin)


LRU_ROW_TILE = 256
LRU_LOOP_UNROLL = 8


def _lru_gates(xc, wg_ref, ba_ref, bx_ref, nsp_ref, g):
    z = _dot(xc.astype(BF16), wg_ref[g])
    r = 1.0 / (1.0 + jnp.exp2(z[:, :LANES] + ba_ref[:, g * LANES:(g + 1) * LANES]))
    i = 1.0 / (1.0 + jnp.exp2(z[:, LANES:] + bx_ref[:, g * LANES:(g + 1) * LANES]))
    a = jnp.exp2(r * nsp_ref[:, g * LANES:(g + 1) * LANES])
    y = 1.0 - a * a
    root = jnp.where(y > 0.0, y * lax.rsqrt(y), 0.0)
    return a, root * (i * xc)


def _lru_fwd_kernel(x_ref, prev_ref, next_ref, cw_ref, cb_ref, wg_ref, ba_ref, bx_ref, nsp_ref,
                    xc_ref, hf_ref, xs_scr, a_scr, u_scr, h_scr):
    i = pl.program_id(0)
    nblk = pl.num_programs(0)
    nb, tc, _ = x_ref.shape
    rows = tc * nb
    halo_rows = HALO * nb
    ntile = rows // LRU_ROW_TILE
    steps_per_tile = LRU_ROW_TILE // nb

    @pl.when(i == 0)
    def _():
        h_scr[...] = jnp.zeros_like(h_scr)

    has_prev = (i > 0).astype(F32)
    has_next = (i < nblk - 1).astype(F32)
    for g in range(LRU_GROUPS):
        lanes = slice(g * LANES, (g + 1) * LANES)
        for b in range(nb):
            xs_scr[g, pl.ds(b, HALO, stride=nb), :] = prev_ref[b, :, lanes] * has_prev
            xs_scr[g, pl.ds(halo_rows + rows + b, HALO, stride=nb), :] = next_ref[b, :, lanes] * has_next
            if g == 0:
                xs_scr[g, pl.ds(halo_rows + b, tc, stride=nb), :] = x_ref[b, :, lanes]

    def to_time_major(k, g):
        lanes = slice(g * LANES, (g + 1) * LANES)
        t0 = pl.multiple_of(k * steps_per_tile, steps_per_tile)
        r0 = pl.multiple_of(k * LRU_ROW_TILE, LRU_ROW_TILE)
        for b in range(nb):
            xs_scr[g, pl.ds(halo_rows + r0 + b, steps_per_tile, stride=nb), :] = (
                x_ref[b, pl.ds(t0, steps_per_tile), lanes])

    left = CONV_WIDTH // 2

    def coeffs(k, g):
        lanes = slice(g * LANES, (g + 1) * LANES)
        r0 = pl.multiple_of(k * LRU_ROW_TILE, LRU_ROW_TILE)
        xc = jnp.broadcast_to(cb_ref[:, lanes], (LRU_ROW_TILE, LANES))
        for j in range(CONV_WIDTH):
            off = halo_rows + (j - left) * nb
            xc = xc + xs_scr[g, pl.ds(r0 + off, LRU_ROW_TILE), :] * cw_ref[j:j + 1, lanes]
        a, u = _lru_gates(xc, wg_ref, ba_ref, bx_ref, nsp_ref, g)
        xc_ref[pl.ds(r0, LRU_ROW_TILE), lanes] = xc
        a_scr[g, pl.ds(r0, LRU_ROW_TILE), :] = a
        u_scr[g, pl.ds(r0, LRU_ROW_TILE), :] = u

    def scan(k, g, h):
        lanes = slice(g * LANES, (g + 1) * LANES)
        r0 = pl.multiple_of(k * LRU_ROW_TILE, LRU_ROW_TILE)
        for t in range(steps_per_tile):
            r = r0 + t * nb
            h = a_scr[g, pl.ds(r, nb), :] * h + u_scr[g, pl.ds(r, nb), :]
            hf_ref[pl.ds(r, nb), lanes] = h
        return h

    for p in range(LRU_GROUPS + 1):
        def body(k, h, p=p):
            if p < LRU_GROUPS:
                coeffs(k, p)
            if p >= 1:
                h = scan(k, p - 1, h)
            if p + 1 < LRU_GROUPS:
                to_time_major(k, p + 1)
            return h

        h0 = h_scr[p - 1] if p >= 1 else jnp.zeros((nb, LANES), F32)
        h = lax.fori_loop(0, ntile, body, h0, unroll=LRU_LOOP_UNROLL)
        if p >= 1:
            h_scr[p - 1] = h


def _lru_bwd_kernel(xc_ref, hf_ref, gate_ref, wg_ref, ba_ref, bx_ref, nsp_ref,
                    rk_ref, rv_ref, kb_ref, gch_ref, bd_ref,
                    y_ref, st_ref, a_scr, hb_scr, h_scr, s_scr, *, ret_steps_per_batch):
    i = pl.program_id(0)
    nb, tc, _ = gate_ref.shape
    rows = tc * nb
    ntile = rows // LRU_ROW_TILE
    steps_per_tile = LRU_ROW_TILE // nb

    @pl.when(i == 0)
    def _():
        h_scr[...] = jnp.zeros_like(h_scr)

    _ret_bwd_body(i % ret_steps_per_batch == 0, rk_ref, rv_ref, kb_ref, gch_ref, bd_ref, st_ref, s_scr)

    def coeffs(k, g):
        lanes = slice(g * LANES, (g + 1) * LANES)
        r0 = pl.multiple_of(k * LRU_ROW_TILE, LRU_ROW_TILE)
        a, u = _lru_gates(xc_ref[pl.ds(r0, LRU_ROW_TILE), lanes], wg_ref, ba_ref, bx_ref, nsp_ref, g)
        a_scr[g, pl.ds(r0, LRU_ROW_TILE), :] = a
        hb_scr[g, pl.ds(r0, LRU_ROW_TILE), :] = u

    def scan(k, g, h):
        lanes = slice(g * LANES, (g + 1) * LANES)
        r0 = pl.multiple_of(k * LRU_ROW_TILE, LRU_ROW_TILE)
        for t in reversed(range(steps_per_tile)):
            r = r0 + t * nb
            h = a_scr[g, pl.ds(r, nb), :] * h + hb_scr[g, pl.ds(r, nb), :]
            hb_scr[g, pl.ds(r, nb), :] = h + hf_ref[pl.ds(r, nb), lanes]
        return h

    def to_batch_major(k, g):
        lanes = slice(g * LANES, (g + 1) * LANES)
        t0 = pl.multiple_of(k * steps_per_tile, steps_per_tile)
        r0 = pl.multiple_of(k * LRU_ROW_TILE, LRU_ROW_TILE)
        for b in range(nb):
            hsum = hb_scr[g, pl.ds(r0 + b, steps_per_tile, stride=nb), :]
            gate = gate_ref[b, pl.ds(t0, steps_per_tile), lanes].astype(F32)
            y_ref[b, pl.ds(t0, steps_per_tile), lanes] = (hsum * gate).astype(y_ref.dtype)

    for p in range(LRU_GROUPS + 2):
        def body(kk, h, p=p):
            k = ntile - 1 - kk
            if p < LRU_GROUPS:
                coeffs(k, p)
            if 1 <= p <= LRU_GROUPS:
                h = scan(k, p - 1, h)
            if p >= 2:
                to_batch_major(k, p - 2)
            return h

        scanning = 1 <= p <= LRU_GROUPS
        h0 = h_scr[p - 1] if scanning else jnp.zeros((nb, LANES), F32)
        h = lax.fori_loop(0, ntile, body, h0, unroll=LRU_LOOP_UNROLL)
        if scanning:
            h_scr[p - 1] = h


def _lru_mixer(lx3, lg3, cw, cb, wg, ba, bx, nsp, tc, rqkv3, tabs):
    nb, seq, _ = lx3.shape
    assert nb == SUBLANES, "the RG-LRU kernels put the batch on the sublane axis"
    nblk = seq // tc
    rows = tc * nb
    hb = tc // HALO
    const2 = lambda i: (0, 0)
    const3 = lambda i: (0, 0, 0)
    xc_tm, hf_tm = pl.pallas_call(
        _lru_fwd_kernel,
        grid=(nblk,),
        in_specs=[pl.BlockSpec((nb, tc, LRU_WIDTH), lambda i: (0, i, 0)),
                  pl.BlockSpec((nb, HALO, LRU_WIDTH), lambda i: (0, jnp.maximum(i * hb - 1, 0), 0)),
                  pl.BlockSpec((nb, HALO, LRU_WIDTH),
                               lambda i: (0, jnp.minimum((i + 1) * hb, seq // HALO - 1), 0)),
                  pl.BlockSpec(cw.shape, const2), pl.BlockSpec(cb.shape, const2),
                  pl.BlockSpec(wg.shape[1:], const3), pl.BlockSpec(ba.shape[1:], const2),
                  pl.BlockSpec(bx.shape[1:], const2), pl.BlockSpec(nsp.shape[1:], const2)],
        out_specs=[pl.BlockSpec((rows, LRU_WIDTH), lambda i: (i, 0)),
                   pl.BlockSpec((rows, LRU_WIDTH), lambda i: (i, 0))],
        out_shape=[jax.ShapeDtypeStruct((seq * nb, LRU_WIDTH), F32),
                   jax.ShapeDtypeStruct((seq * nb, LRU_WIDTH), F32)],
        scratch_shapes=[pltpu.VMEM((LRU_GROUPS, (tc + 2 * HALO) * nb, LANES), F32),
                        pltpu.VMEM((LRU_GROUPS, rows, LANES), F32),
                        pltpu.VMEM((LRU_GROUPS, rows, LANES), F32),
                        pltpu.VMEM((LRU_GROUPS, nb, LANES), F32)],
        compiler_params=_params("arbitrary"),
        name="lru_fwd",
    )(lx3, lx3, lx3, cw, cb, wg[0], ba[0], bx[0], nsp[0])

    rev = lambda i: (nblk - 1 - i, 0)
    spb = nblk // nb
    tr = seq // spb
    ret_col = lambda cidx: pl.BlockSpec((None, tr, RET_WIDTH), lambda i: (i // spb, spb - 1 - i % spb, cidx))
    tab = lambda name: pl.BlockSpec(tabs[name].shape, lambda i: (0,) * tabs[name].ndim)
    return pl.pallas_call(
        functools.partial(_lru_bwd_kernel, ret_steps_per_batch=spb),
        grid=(nblk,),
        in_specs=[pl.BlockSpec((rows, LRU_WIDTH), rev),
                  pl.BlockSpec((rows, LRU_WIDTH), rev),
                  pl.BlockSpec((nb, tc, LRU_WIDTH), lambda i: (0, nblk - 1 - i, 0)),
                  pl.BlockSpec(wg.shape[1:], const3), pl.BlockSpec(ba.shape[1:], const2),
                  pl.BlockSpec(bx.shape[1:], const2), pl.BlockSpec(nsp.shape[1:], const2),
                  ret_col(1), ret_col(2), tab('k_b'), tab('gch'), tab('bd')],
        out_specs=[pl.BlockSpec((nb, tc, LRU_WIDTH), lambda i: (0, nblk - 1 - i, 0)),
                   pl.BlockSpec((None, tr // CHUNK, RET_GROUPS, LANES, LANES),
                                lambda i: (i // spb, spb - 1 - i % spb, 0, 0, 0))],
        out_shape=[jax.ShapeDtypeStruct((nb, seq, LRU_WIDTH), BF16),
                   jax.ShapeDtypeStruct((nb, seq // CHUNK, RET_GROUPS, LANES, LANES), BF16)],
        scratch_shapes=[pltpu.VMEM((LRU_GROUPS, rows, LANES), F32),
                        pltpu.VMEM((LRU_GROUPS, rows, LANES), F32),
                        pltpu.VMEM((LRU_GROUPS, nb, LANES), F32),
                        pltpu.VMEM((RET_GROUPS, LANES, LANES), F32)],
        compiler_params=_params("arbitrary"),
        name="lru_bwd_ret_bwd",
    )(xc_tm, hf_tm, lg3, wg[1], ba[1], bx[1], nsp[1], rqkv3, rqkv3, tabs['k_b'], tabs['gch'], tabs['bd'])


def _lru_weights(w_a, b_a, w_x, b_x, lam):
    nblocks = w_a.shape[1]
    per_group = LANES // w_a.shape[2]

    def blockdiag(w):
        w = w.reshape(2, nblocks // per_group, per_group, w.shape[2], w.shape[3])
        eye = jnp.eye(per_group, dtype=w.dtype)
        w = w[:, :, :, :, None, :] * eye[None, None, :, None, :, None]
        return w.reshape(2, nblocks // per_group, LANES, LANES)

    log2e = math.log2(math.e)
    wg = (jnp.concatenate([blockdiag(w_a), blockdiag(w_x)], axis=-1) * -log2e).astype(BF16)
    nsp = (-LRU_C * log2e) * jax.nn.softplus(-lam.astype(F32))
    return (wg, (b_a.astype(F32) * -log2e)[:, None, :], (b_x.astype(F32) * -log2e)[:, None, :],
            nsp[:, None, :])


def _ret_tables(seq):
    c = CHUNK
    log_g = jnp.log1p(-jnp.exp2(-5.0 - jnp.arange(RET_HEADS, dtype=F32)))
    lg_lane = jnp.repeat(log_g, HEAD_DIM)[None, :]
    idx = jnp.arange(c, dtype=F32)[:, None]
    t = {}
    t['q_f'] = jnp.exp((idx + 1.0) * lg_lane)
    t['q_b'] = jnp.exp((c - idx) * lg_lane)
    t['k_f'] = jnp.exp((c - 1 - idx) * lg_lane)
    t['k_b'] = jnp.exp(idx * lg_lane)
    dec = jnp.exp(jnp.abs(idx - idx.T)[None] * log_g[:, None, None])
    t['decay'] = dec.reshape(RET_GROUPS, 2, c, c).transpose(0, 2, 1, 3).reshape(RET_GROUPS, c, 2 * c)
    li = np.arange(LANES)
    same = jnp.asarray((li[:, None] // HEAD_DIM) == (li[None, :] // HEAD_DIM), F32)
    g_chunk = jnp.exp(c * log_g).reshape(RET_GROUPS, LANES // HEAD_DIM)
    t['gch'] = jnp.repeat(g_chunk, HEAD_DIM, axis=1)[:, :, None] * same[None]
    t['bd'] = same[None]
    half = HEAD_DIM // 2
    inv_freq = ROPE_BASE ** (-jnp.arange(half, dtype=F32) / half)
    ang = jnp.arange(seq, dtype=F32)[:, None] * inv_freq[None, :]
    cos, sin = jnp.cos(ang), jnp.sin(ang)
    reps = LANES // HEAD_DIM
    t['cos'] = jnp.tile(jnp.concatenate([cos, cos], axis=1), (1, reps))
    t['sin'] = jnp.tile(jnp.concatenate([-sin, sin], axis=1), (1, reps))
    return t


def _rope(t, cos, sin):
    half = HEAD_DIM // 2
    lane = lax.broadcasted_iota(jnp.int32, t.shape, 1)
    first = (lane % HEAD_DIM) < half
    partner = jnp.where(first, pltpu.roll(t, LANES - half, 1), pltpu.roll(t, half, 1))
    return t * cos + partner * sin


def _split_heads(t, lo):
    zero = jnp.zeros_like(t)
    return jnp.concatenate([jnp.where(lo, t, zero), jnp.where(lo, zero, t)], axis=0)


def _ret_blk(ref, c, g):
    return ref[c * CHUNK:(c + 1) * CHUNK, g * LANES:(g + 1) * LANES]


def _ret_bwd_body(first, k_ref, v_ref, kb_ref, gch_ref, bd_ref, st_ref, s_scr):
    @pl.when(first)
    def _():
        s_scr[...] = jnp.zeros_like(s_scr)

    nchunk = k_ref.shape[0] // CHUNK
    kv = {}
    for c in range(nchunk):
        for g in range(RET_GROUPS):
            kb = kb_ref[:, g * LANES:(g + 1) * LANES]
            kv[c, g] = _dot_tn((_ret_blk(k_ref, c, g).astype(F32) * kb).astype(BF16), _ret_blk(v_ref, c, g))
    for g in range(RET_GROUPS):
        st = s_scr[g]
        for c in reversed(range(nchunk)):
            st_ref[c, g] = st.astype(st_ref.dtype)
            st = gch_ref[g] * st + bd_ref[0] * kv[c, g]
        s_scr[g] = st


def _ret_fwd_kernel(q_ref, k_ref, v_ref, g_ref, qf_ref, qb_ref, kf_ref, dec_ref,
                    gch_ref, bd_ref, gnw_ref, stb_ref, y_ref, s_scr):
    j = pl.program_id(1)

    @pl.when(j == 0)
    def _():
        s_scr[...] = jnp.zeros_like(s_scr)

    nchunk = q_ref.shape[0] // CHUNK
    units = [(c, g) for c in range(nchunk) for g in range(RET_GROUPS)]
    lo = lax.broadcasted_iota(jnp.int32, (CHUNK, LANES), 1) < HEAD_DIM
    gn_avg = bd_ref[0].astype(BF16)

    s, kv = {}, {}
    for c, g in units:
        lanes = slice(g * LANES, (g + 1) * LANES)
        k = _ret_blk(k_ref, c, g)
        s[c, g] = _dot_nt(_ret_blk(q_ref, c, g), _split_heads(k.astype(F32), lo).astype(BF16))
        kv[c, g] = _dot_tn((k.astype(F32) * kf_ref[:, lanes]).astype(BF16), _ret_blk(v_ref, c, g))
    o = {}
    for c, g in units:
        v2 = _split_heads(_ret_blk(v_ref, c, g).astype(F32), lo).astype(BF16)
        o[c, g] = _dot((s[c, g] * dec_ref[g]).astype(BF16), v2)
    for g in range(RET_GROUPS):
        lanes = slice(g * LANES, (g + 1) * LANES)
        st = s_scr[g]
        for c in range(nchunk):
            q = _ret_blk(q_ref, c, g).astype(F32)
            q2 = jnp.concatenate([(q * qf_ref[:, lanes]).astype(BF16), (q * qb_ref[:, lanes]).astype(BF16)],
                                 axis=1)
            st2 = jnp.concatenate([st.astype(BF16), stb_ref[c, g]], axis=0)
            o[c, g] = o[c, g] + _dot(q2, st2)
            st = gch_ref[g] * st + bd_ref[0] * kv[c, g]
        s_scr[g] = st
    for g in range(RET_GROUPS):
        lanes = slice(g * LANES, (g + 1) * LANES)
        og = jnp.concatenate([o[c, g] for c in range(nchunk)], axis=0)
        oc = og - _dot(og.astype(BF16), gn_avg) * (1.0 / HEAD_DIM)
        var = _dot((oc * oc).astype(BF16), gn_avg) * (1.0 / HEAD_DIM)
        y = g_ref[:, lanes].astype(F32) * (oc * lax.rsqrt(var + GN_EPS) * gnw_ref[:, lanes])
        y_ref[:, lanes] = y.astype(y_ref.dtype)


def _ret_mixer(rqkv3, rg3, st_b, tabs, gn_w, tb):
    nb, seq, _ = rqkv3.shape
    nblk = seq // tb
    cpb = tb // CHUNK
    w = RET_WIDTH
    const2 = lambda b, j: (0, 0)
    const3 = lambda b, j: (0, 0, 0)
    tab2 = lambda name: pl.BlockSpec(tabs[name].shape, const2)
    tab3 = lambda name: pl.BlockSpec(tabs[name].shape, const3)

    col = lambda cidx: pl.BlockSpec((None, tb, w), lambda b, j: (b, j, cidx))

    return pl.pallas_call(
        _ret_fwd_kernel,
        grid=(nb, nblk),
        in_specs=[col(0), col(1), col(2), col(0),
                  tab2('q_f'), tab2('q_b'), tab2('k_f'), tab3('decay'), tab3('gch'), tab3('bd'),
                  pl.BlockSpec(gn_w.shape, const2),
                  pl.BlockSpec((None, cpb, RET_GROUPS, LANES, LANES), lambda b, j: (b, j, 0, 0, 0))],
        out_specs=pl.BlockSpec((None, tb, w), lambda b, j: (b, j, 0)),
        out_shape=jax.ShapeDtypeStruct((nb, seq, w), BF16),
        scratch_shapes=[pltpu.VMEM((RET_GROUPS, LANES, LANES), F32)],
        compiler_params=_params("parallel", "arbitrary"),
        name="ret_fwd",
    )(rqkv3, rqkv3, rqkv3, rg3, tabs['q_f'], tabs['q_b'], tabs['k_f'],
      tabs['decay'], tabs['gch'], tabs['bd'], gn_w, st_b)


def _na_bias_table(rpb, rows_n):
    kh = min(NA_MAX_KH, rows_n)
    var = np.arange(kh)
    j = np.arange(kh)
    dr = j[None, :] - var[:, None]
    ri = dr + NA_MAX_KH - 1
    c = np.arange(GRID_W)
    cstart = np.clip(c - NA_KW // 2, 0, GRID_W - NA_KW)
    kc = np.arange(GRID_W)
    dc = kc[None, :] - c[:, None]
    ci = np.clip(dc, 1 - NA_KW, NA_KW - 1) + NA_KW - 1
    valid = (kc[None, :] >= cstart[:, None]) & (kc[None, :] < cstart[:, None] + NA_KW)
    col_sel = np.zeros((GRID_W, GRID_W, rpb.shape[3]), np.float32)
    col_sel[c[:, None], kc[None, :], ci] = 1.0
    t = jnp.einsum('lhab,qkb->lhaqk', rpb.astype(F32), col_sel, precision=lax.Precision.HIGHEST)
    t = jnp.where(valid[None, None, None], t, -jnp.inf)
    b = jnp.stack([jnp.concatenate([t[:, :, ri[v, jj]] for jj in range(kh)], axis=-1) for v in range(kh)],
                  axis=1)
    return b.reshape(rpb.shape[0], kh, NA_HEADS * GRID_W, kh * GRID_W)


def _na_kernel(q_ref, k_ref, v_ref, bias_ref, y_ref, *, rows_per_step, rows_n):
    jb = pl.program_id(1)
    kh = min(NA_MAX_KH, rows_n)
    slab = kh * GRID_W
    lane = lax.broadcasted_iota(jnp.int32, (GRID_W, NA_WIDTH), 1)
    zero = jnp.zeros((GRID_W, NA_WIDTH), q_ref.dtype)
    head_of_lane = [(lane >= h * HEAD_DIM) & (lane < (h + 1) * HEAD_DIM) for h in range(NA_HEADS)]

    k0, s = [], []
    for rr in range(rows_per_step):
        r = jb * rows_per_step + rr
        rstart = jnp.clip(r - kh // 2, 0, rows_n - kh)
        k0.append(pl.multiple_of(rstart * GRID_W, GRID_W))
        q = q_ref[rr * GRID_W:(rr + 1) * GRID_W, :] * (HEAD_DIM ** -0.5)
        q4 = jnp.concatenate([jnp.where(m, q, zero) for m in head_of_lane], axis=0)
        s.append(_dot_nt(q4, k_ref[pl.ds(k0[rr], slab), :]) + bias_ref[r - rstart])
    p, l = [], []
    for rr in range(rows_per_step):
        e = jnp.exp(s[rr] - jnp.max(s[rr], axis=1, keepdims=True))
        l.append(jnp.sum(e, axis=1, keepdims=True))
        p.append(e.astype(BF16))
    for rr in range(rows_per_step):
        o4 = _dot(p[rr], v_ref[pl.ds(k0[rr], slab), :]) / l[rr]
        o = o4[(NA_HEADS - 1) * GRID_W:]
        for h in reversed(range(NA_HEADS - 1)):
            o = jnp.where(lane < (h + 1) * HEAD_DIM, o4[h * GRID_W:(h + 1) * GRID_W], o)
        y_ref[rr * GRID_W:(rr + 1) * GRID_W, :] = o.astype(y_ref.dtype)


def _na_mixer(na3, bias, layer, rows_per_step):
    nb, seq, _ = na3.shape
    rows_n = seq // GRID_W
    tq = rows_per_step * GRID_W
    kern = functools.partial(_na_kernel, rows_per_step=rows_per_step, rows_n=rows_n)
    return pl.pallas_call(
        kern,
        grid=(nb, rows_n // rows_per_step),
        in_specs=[pl.BlockSpec((None, tq, NA_WIDTH), lambda b, j: (b, j, 0)),
                  pl.BlockSpec((None, seq, NA_WIDTH), lambda b, j: (b, 0, 1)),
                  pl.BlockSpec((None, seq, NA_WIDTH), lambda b, j: (b, 0, 2)),
                  pl.BlockSpec((None,) + bias.shape[1:], lambda b, j: (layer, 0, 0, 0))],
        out_specs=pl.BlockSpec((None, tq, NA_WIDTH), lambda b, j: (b, j, 0)),
        out_shape=jax.ShapeDtypeStruct((nb, seq, NA_WIDTH), BF16),
        compiler_params=_params("parallel", "arbitrary"),
        name="na",
    )(na3, na3, na3, bias)


def _layer_norm(z, g, b):
    mu = jnp.mean(z, axis=-1, keepdims=True)
    zc = z - mu
    var = jnp.mean(zc * zc, axis=-1, keepdims=True)
    return zc * lax.rsqrt(var + LN_EPS) * g + b


FFN_CHUNK = 256


def _mix_ffn_kernel(x_ref, yl_ref, yr_ref, yn_ref, wo_ref, g1_ref, b1_ref, wg_ref, wu_ref, wd_ref,
                    g2_ref, b2_ref, o_ref, acc_ref, *, alpha):
    y = jnp.concatenate([yl_ref[...], yr_ref[...], yn_ref[...]], axis=1)
    x1 = _layer_norm(alpha * x_ref[...] + _dot(y, wo_ref[...]), g1_ref[...], b1_ref[...])
    xb = x1.astype(BF16)
    acc_ref[...] = alpha * x1
    nchunk = wg_ref.shape[1] // FFN_CHUNK
    for c in range(nchunk):
        cols = slice(c * FFN_CHUNK, (c + 1) * FFN_CHUNK)
        gate = _dot(xb, wg_ref[:, cols])
        up = _dot(xb, wu_ref[:, cols])
        hid = (gate * _sigmoid(gate) * up).astype(BF16)
        acc_ref[...] += _dot(hid, wd_ref[cols, :])
    o_ref[...] = _layer_norm(acc_ref[...], g2_ref[...], b2_ref[...])


def _mix_ffn(x2, yl, yr, yn, wo_bf, g1, b1, wg_bf, wu_bf, wd_bf, g2, b2, layer, alpha, tm):
    n, d = x2.shape
    row = lambda width: pl.BlockSpec((tm, width), lambda i: (i, 0))
    const = lambda a: pl.BlockSpec((None,) + a.shape[1:], lambda i: (layer, 0, 0),
                                   pipeline_mode=pl.Buffered(1))
    return pl.pallas_call(
        functools.partial(_mix_ffn_kernel, alpha=alpha),
        grid=(n // tm,),
        in_specs=[row(d), row(yl.shape[1]), row(yr.shape[1]), row(yn.shape[1]),
                  const(wo_bf), const(g1), const(b1), const(wg_bf), const(wu_bf), const(wd_bf),
                  const(g2), const(b2)],
        out_specs=row(d),
        out_shape=jax.ShapeDtypeStruct((n, d), F32),
        scratch_shapes=[pltpu.VMEM((tm, d), F32)],
        compiler_params=_params("parallel"),
        name="mix_ffn_ln",
    )(x2, yl, yr, yn, wo_bf, g1, b1, wg_bf, wu_bf, wd_bf, g2, b2)


ROW_TILE = 1024
LRU_TIME_BLOCK = 256
RET_TIME_BLOCK = 2048
NA_ROWS_PER_STEP = 32


def kernel(x, w_in, conv_w, conv_b, lru_w_a, lru_b_a, lru_w_x, lru_b_x, lru_lam, ret_gn_w, na_rpb,
           w_out, ln1_g, ln1_b, w_gate, w_up, w_down, ln2_g, ln2_b):
    nb, seq, d = x.shape
    depth = w_in.shape[0]
    alpha = (2 * depth) ** 0.25
    n = nb * seq
    tm = min(ROW_TILE, n)
    tabs = _ret_tables(seq)
    bias = _na_bias_table(na_rpb, seq // GRID_W)
    w_in_bf, w_out_bf = w_in.astype(BF16), w_out.astype(BF16)
    w_gate_bf, w_up_bf, w_down_bf = w_gate.astype(BF16), w_up.astype(BF16), w_down.astype(BF16)
    vec = lambda a: a.astype(F32)[:, None, :]
    ln = [vec(a) for a in (ln1_g, ln1_b, ln2_g, ln2_b)]
    x2 = x.reshape(n, d).astype(F32)
    for l in range(depth):
        lx2, lg2, rqkv2, rg2, na2 = _in_proj(x2, w_in_bf, l, tabs['cos'], tabs['sin'], tm)
        wg, ba, bx, nsp = _lru_weights(lru_w_a[l], lru_b_a[l], lru_w_x[l], lru_b_x[l], lru_lam[l])
        rqkv3 = rqkv2.reshape(nb, seq, -1)
        y_lru, st_b = _lru_mixer(lx2.reshape(nb, seq, -1), lg2.reshape(nb, seq, -1), conv_w[l].astype(F32),
                                 conv_b[l].astype(F32)[None, :], wg, ba, bx, nsp, min(LRU_TIME_BLOCK, seq),
                                 rqkv3, tabs)
        y_ret = _ret_mixer(rqkv3, rg2.reshape(nb, seq, -1), st_b, tabs,
                           ret_gn_w[l].astype(F32)[None, :], min(RET_TIME_BLOCK, seq))
        y_na = _na_mixer(na2.reshape(nb, seq, -1), bias, l, NA_ROWS_PER_STEP)
        x2 = _mix_ffn(x2, y_lru.reshape(n, -1), y_ret.reshape(n, -1), y_na.reshape(n, -1),
                      w_out_bf, ln[0], ln[1], w_gate_bf, w_up_bf, w_down_bf, ln[2], ln[3], l, alpha, tm)
    return x2.reshape(nb, seq, d).astype(x.dtype)
```
